```python
import jax, jax.numpy as jnp
from jax import lax
import numpy as np

D_MODEL = 1024
BATCH = 2
SEQ = 16384
DEPTH = 1
DEC_BATCH = 8
DEC_SEQ = 32
PAST_LEN = 2048

CHUNK = 64
MIX_WIDTH = D_MODEL
CONV_CH = MIX_WIDTH // 2
CONV_WIDTH = 31
HEAD_DIM = 64
N_HEADS = (MIX_WIDTH - CONV_CH) // HEAD_DIM
N_KV_HEADS = 2
GROUP = N_HEADS // N_KV_HEADS
WINDOW = 128
WIN_CHUNKS = WINDOW // CHUNK
ROPE_THETA = 10000.0
D_FF = 4 * D_MODEL
D_PLE = 256
EPS = 1e-6
NEG = -1e30
Q_COLS = N_HEADS * HEAD_DIM
KV_COLS = N_KV_HEADS * HEAD_DIM
IN_COLS = 2 * CONV_CH + Q_COLS + 2 * KV_COLS
SPLITS = [CONV_CH, 2 * CONV_CH, 2 * CONV_CH + Q_COLS, 2 * CONV_CH + Q_COLS + KV_COLS]

kernel_name = 'hybrid_conformer_conv_swa_sink_stream'


def rmsnorm(x, g):
    xf = x.astype(jnp.float32)
    y = xf * lax.rsqrt(jnp.mean(xf * xf, axis=-1, keepdims=True) + EPS)
    return (y * g.astype(jnp.float32)).astype(x.dtype)


def layernorm(x, g, b):
    xf = x.astype(jnp.float32)
    mu = jnp.mean(xf, axis=-1, keepdims=True)
    xc = xf - mu
    y = xc * lax.rsqrt(jnp.mean(xc * xc, axis=-1, keepdims=True) + EPS)
    return (y * g.astype(jnp.float32) + b.astype(jnp.float32)).astype(x.dtype)


def rope(x, pos):
    half = HEAD_DIM // 2
    inv = 1.0 / (ROPE_THETA ** (jnp.arange(half, dtype=jnp.float32) / half))
    ang = pos[:, None] * inv[None, :]
    cos = jnp.cos(ang)[:, None, :]
    sin = jnp.sin(ang)[:, None, :]
    xf = x.astype(jnp.float32)
    x1, x2 = xf[..., :half], xf[..., half:]
    return jnp.concatenate([x1 * cos - x2 * sin, x2 * cos + x1 * sin], axis=-1).astype(x.dtype)


def sink_softmax(s, sink):
    m = jnp.maximum(jnp.max(s, axis=-1, keepdims=True), sink)
    e = jnp.exp(s - m)
    return e / (jnp.sum(e, axis=-1, keepdims=True) + jnp.exp(sink - m))


def window_attn_prompt(q, k, v, sink):
    B, S = q.shape[0], q.shape[1]
    n_c = S // CHUNK
    qb = q.reshape(B, n_c, CHUNK, N_KV_HEADS, GROUP, HEAD_DIM)
    pad = ((0, 0), (WIN_CHUNKS * CHUNK, 0), (0, 0), (0, 0))
    kp = jnp.pad(k, pad).reshape(B, n_c + WIN_CHUNKS, CHUNK, N_KV_HEADS, HEAD_DIM)
    vp = jnp.pad(v, pad).reshape(B, n_c + WIN_CHUNKS, CHUNK, N_KV_HEADS, HEAD_DIM)
    kw = jnp.concatenate([kp[:, j:j + n_c] for j in range(WIN_CHUNKS + 1)], axis=2)
    vw = jnp.concatenate([vp[:, j:j + n_c] for j in range(WIN_CHUNKS + 1)], axis=2)
    s = jnp.einsum('bnckgd,bnskd->bnkgcs', qb, kw, preferred_element_type=jnp.float32) * (HEAD_DIM ** -0.5)
    key_chunk = (jnp.arange(n_c)[:, None] - WIN_CHUNKS
                 + (jnp.arange((WIN_CHUNKS + 1) * CHUNK) // CHUNK)[None, :])
    valid = key_chunk >= 0
    s = jnp.where(valid[None, :, None, None, None, :], s, NEG)
    sk = sink.astype(jnp.float32).reshape(1, 1, N_KV_HEADS, GROUP, 1, 1)
    pr = sink_softmax(s, sk).astype(v.dtype)
    o = jnp.einsum('bnkgcs,bnskd->bnckgd', pr, vw)
    return o.reshape(B, S, N_HEADS * HEAD_DIM)


def window_attn_sample(q, k_hist, v_hist, k, v, sink):
    B, T = q.shape[0], q.shape[1]
    kall = jnp.concatenate([k_hist.astype(k.dtype), k], axis=1)
    vall = jnp.concatenate([v_hist.astype(v.dtype), v], axis=1)
    qb = q.reshape(B, T, N_KV_HEADS, GROUP, HEAD_DIM)
    s = jnp.einsum('btkgd,bskd->bkgts', qb, kall, preferred_element_type=jnp.float32) * (HEAD_DIM ** -0.5)
    sk = sink.astype(jnp.float32).reshape(1, N_KV_HEADS, GROUP, 1, 1)
    pr = sink_softmax(s, sk).astype(v.dtype)
    o = jnp.einsum('bkgts,bskd->btkgd', pr, vall)
    return o.reshape(B, T, N_HEADS * HEAD_DIM)


def trunk_layer(h, p_l, conv_hist, pos, attend, ln_mix_g, w_in, conv_w, conv_b, conv_norm_g,
                conv_norm_b, w_out, ln_ffn_g, w_ff1, w_ff2, ple_norm_g, w_ple_gate, w_ple_proj):
    B, T = h.shape[0], h.shape[1]
    a = rmsnorm(h, ln_mix_g)
    z = a @ w_in
    c_val, c_gate, q, k, v = jnp.split(z, SPLITS, axis=-1)
    u = c_val * jax.nn.sigmoid(c_gate)
    u_ext = jnp.concatenate([conv_hist.astype(u.dtype), u], axis=1)
    dc = lax.conv_general_dilated(u_ext, conv_w[:, None, :].astype(u.dtype), (1,), 'VALID',
                                  dimension_numbers=('NWC', 'WIO', 'NWC'),
                                  feature_group_count=CONV_CH) + conv_b
    c_out = jax.nn.silu(layernorm(dc, conv_norm_g, conv_norm_b))
    q = rope(q.reshape(B, T, N_HEADS, HEAD_DIM), pos)
    k = rope(k.reshape(B, T, N_KV_HEADS, HEAD_DIM), pos)
    v = v.reshape(B, T, N_KV_HEADS, HEAD_DIM)
    att = attend(q, k, v)
    h = h + jnp.concatenate([c_out, att], axis=-1) @ w_out
    f = rmsnorm(h, ln_ffn_g)
    h = h + jnp.square(jax.nn.relu(f @ w_ff1)) @ w_ff2
    gate = jax.nn.sigmoid(rmsnorm(h, ple_norm_g) @ w_ple_gate)
    h = h + gate * (p_l @ w_ple_proj)
    new_conv = u_ext[:, -(CONV_WIDTH - 1):]
    return h, new_conv, k, v


def setup_inputs(seed: int = 0) -> dict:
    key = jax.random.key(seed)
    ks = jax.random.split(key, 24)
    f32 = jnp.float32
    nrm = lambda k, shape, scale: jax.random.normal(k, shape, f32) * scale
    return {
        'x_prompt': nrm(ks[0], (BATCH, SEQ, D_MODEL), 1.0),
        'x_sample': nrm(ks[1], (DEC_BATCH, DEC_SEQ, D_MODEL), 1.0),
        'p_prompt': nrm(ks[2], (DEPTH, BATCH, SEQ, D_PLE), 1.0),
        'p_sample': nrm(ks[3], (DEPTH, DEC_BATCH, DEC_SEQ, D_PLE), 1.0),
        'cache_k': nrm(ks[4], (DEPTH, DEC_BATCH, WINDOW, N_KV_HEADS, HEAD_DIM), 1.0),
        'cache_v': nrm(ks[5], (DEPTH, DEC_BATCH, WINDOW, N_KV_HEADS, HEAD_DIM), 1.0),
        'state_conv': nrm(ks[6], (DEPTH, DEC_BATCH, CONV_WIDTH - 1, CONV_CH), 0.5),
        'ln_mix_g': 1.0 + nrm(ks[7], (DEPTH, D_MODEL), 0.02),
        'w_in': nrm(ks[8], (DEPTH, D_MODEL, IN_COLS), D_MODEL ** -0.5),
        'conv_w': nrm(ks[9], (DEPTH, CONV_WIDTH, CONV_CH), CONV_WIDTH ** -0.5),
        'conv_b': nrm(ks[10], (DEPTH, CONV_CH), 0.02),
        'conv_norm_g': 1.0 + nrm(ks[11], (DEPTH, CONV_CH), 0.02),
        'conv_norm_b': nrm(ks[12], (DEPTH, CONV_CH), 0.02),
        'attn_sink': nrm(ks[13], (DEPTH, N_HEADS), 1.0),
        'w_out': nrm(ks[14], (DEPTH, MIX_WIDTH, D_MODEL), MIX_WIDTH ** -0.5),
        'ln_ffn_g': 1.0 + nrm(ks[15], (DEPTH, D_MODEL), 0.02),
        'w_ff1': nrm(ks[16], (DEPTH, D_MODEL, D_FF), D_MODEL ** -0.5),
        'w_ff2': nrm(ks[17], (DEPTH, D_FF, D_MODEL), 0.5 * D_FF ** -0.5),
        'ple_norm_g': 1.0 + nrm(ks[18], (DEPTH, D_MODEL), 0.02),
        'w_ple_gate': nrm(ks[19], (DEPTH, D_MODEL, D_MODEL), D_MODEL ** -0.5),
        'w_ple_proj': nrm(ks[20], (DEPTH, D_PLE, D_MODEL), D_PLE ** -0.5),
        'final_norm_g': 1.0 + nrm(ks[21], (D_MODEL,), 0.02),
    }


def reference(x_prompt, x_sample, p_prompt, p_sample, cache_k, cache_v, state_conv,
              ln_mix_g, w_in, conv_w, conv_b, conv_norm_g, conv_norm_b, attn_sink, w_out,
              ln_ffn_g, w_ff1, w_ff2, ple_norm_g, w_ple_gate, w_ple_proj, final_norm_g):
    b_p, s_p = x_prompt.shape[0], x_prompt.shape[1]
    t_s = x_sample.shape[1]
    pos_p = jnp.arange(s_p, dtype=jnp.float32)
    pos_s = PAST_LEN + jnp.arange(t_s, dtype=jnp.float32)
    hp, hs = x_prompt, x_sample
    nk_p, nv_p, nc_p, nk_s, nv_s, nc_s = [], [], [], [], [], []
    for l in range(DEPTH):
        w = (ln_mix_g[l], w_in[l], conv_w[l], conv_b[l], conv_norm_g[l], conv_norm_b[l], w_out[l],
             ln_ffn_g[l], w_ff1[l], w_ff2[l], ple_norm_g[l], w_ple_gate[l], w_ple_proj[l])
        sink = attn_sink[l]
        zero_hist = jnp.zeros((b_p, CONV_WIDTH - 1, CONV_CH), x_prompt.dtype)
        hp, c_p, k_p, v_p = trunk_layer(
            hp, p_prompt[l], zero_hist, pos_p,
            lambda q, k, v: window_attn_prompt(q, k, v, sink), *w)
        ck, cv = cache_k[l], cache_v[l]
        hs, c_s, k_s, v_s = trunk_layer(
            hs, p_sample[l], state_conv[l], pos_s,
            lambda q, k, v: window_attn_sample(q, ck, cv, k, v, sink), *w)
        nk_p.append(k_p[:, -WINDOW:])
        nv_p.append(v_p[:, -WINDOW:])
        nc_p.append(c_p)
        nk_s.append(k_s)
        nv_s.append(v_s)
        nc_s.append(c_s)
    y_prompt = rmsnorm(hp, final_norm_g)
    y_sample = rmsnorm(hs, final_norm_g)
    return (y_prompt, y_sample, jnp.stack(nk_p), jnp.stack(nv_p), jnp.stack(nc_p),
            jnp.stack(nk_s), jnp.stack(nv_s), jnp.stack(nc_s))
```

```python
import functools

import jax
import jax.numpy as jnp
from jax import lax
from jax.experimental import pallas as pl
from jax.experimental.pallas import tpu as pltpu

D_MODEL = 1024
CONV_CH = 512
CONV_WIDTH = 31
HEAD_DIM = 64
N_HEADS = 8
N_KV_HEADS = 2
GROUP = N_HEADS // N_KV_HEADS
CHUNK = 64
WINDOW = 128
WIN_CHUNKS = WINDOW // CHUNK
ROPE_THETA = 10000.0
D_FF = 4 * D_MODEL
D_PLE = 256
EPS = 1e-6
NEG = -1e30
PAST_LEN = 2048
Q_COLS = N_HEADS * HEAD_DIM
KV_COLS = N_KV_HEADS * HEAD_DIM
IN_COLS = 2 * CONV_CH + Q_COLS + 2 * KV_COLS

LANES = 128
SUBLANES = 8
HIST = 32
HIST_PAD = HIST - (CONV_WIDTH - 1)
TILE = 256
VMEM_LIMIT_BYTES = 56 * 1024 * 1024

F32 = jnp.float32
BF16 = jnp.bfloat16


def _rmsnorm(x, g):
    return x * lax.rsqrt(jnp.mean(x * x, axis=-1, keepdims=True) + EPS) * g


def _layernorm(x, g, b):
    mu = jnp.mean(x, axis=-1, keepdims=True)
    xc = x - mu
    return xc * lax.rsqrt(jnp.mean(xc * xc, axis=-1, keepdims=True) + EPS) * g + b


def _mm(a, w):
    return jnp.dot(a.astype(BF16), w, preferred_element_type=F32)


def _rope(x, cos, sin_signed):
    lane = lax.broadcasted_iota(jnp.int32, x.shape, 1)
    first_half = (lane % HEAD_DIM) < (HEAD_DIM // 2)
    partner = jnp.where(first_half,
                        pltpu.roll(x, LANES - HEAD_DIM // 2, 1),
                        pltpu.roll(x, HEAD_DIM // 2, 1))
    return x * cos + partner * sin_signed


def _depthwise_conv(u_ref, base, nrows, row_block, w_ref, b_ref, out_ref, out_base):
    for r0 in range(0, nrows, row_block):
        for c0 in range(0, CONV_CH, LANES):
            cs = slice(c0, c0 + LANES)
            acc = jnp.broadcast_to(b_ref[:, cs], (row_block, LANES))
            for sub in range(SUBLANES):
                offs = [o for o in range(sub, HIST + 1, SUBLANES) if HIST_PAD <= o <= HIST]
                span = offs[-1] - offs[0]
                win = u_ref[pl.ds(base + r0 + offs[0], row_block + span), cs]
                for o in offs:
                    j = o - HIST_PAD
                    a = o - offs[0]
                    acc = acc + w_ref[j:j + 1, cs] * win[a:a + row_block]
            out_ref[pl.ds(out_base + r0, row_block), cs] = acc


def _softmax_pv(s, sinks, v_win):
    rows = s.shape[0] // GROUP
    es, invs = [], []
    for g in range(GROUP):
        sg = s[g * rows:(g + 1) * rows]
        m = jnp.maximum(jnp.max(sg, axis=-1, keepdims=True), sinks[g])
        e = jnp.exp(sg - m)
        denom = jnp.sum(e, axis=-1, keepdims=True) + jnp.exp(sinks[g] - m)
        es.append(e.astype(BF16))
        invs.append(1.0 / denom)
    o = jnp.dot(jnp.concatenate(es, axis=0), v_win, preferred_element_type=F32)
    return o * jnp.concatenate(invs, axis=0)


def _stack_heads(q, r0, rows, kh):
    parts = []
    for g in range(GROUP):
        h = kh * GROUP + g
        parts.append(q[r0:r0 + rows, h * HEAD_DIM:(h + 1) * HEAD_DIM])
    return (jnp.concatenate(parts, axis=0) * (HEAD_DIM ** -0.5)).astype(BF16)


def _in_proj(x, g_mix_ref, w_in_ref, cos_ref, sin_ref):
    z = _mm(_rmsnorm(x, g_mix_ref[...]), w_in_ref[...])
    u = z[:, :CONV_CH] * jax.nn.sigmoid(z[:, CONV_CH:2 * CONV_CH])
    cos = cos_ref[...]
    sin = sin_ref[...]
    q0 = 2 * CONV_CH
    q = jnp.concatenate(
        [_rope(z[:, q0 + c:q0 + c + LANES], cos, sin) for c in range(0, Q_COLS, LANES)], axis=-1)
    k = _rope(z[:, q0 + Q_COLS:q0 + Q_COLS + KV_COLS], cos, sin)
    v = z[:, q0 + Q_COLS + KV_COLS:]
    return u, q, k, v


def _post_mix(x, c_out, att, p, w_out_ref, g_ffn_ref, w_ff1_ref, w_ff2_ref, g_ple_ref,
              w_gate_ref, w_proj_ref, g_fin_ref):
    mix = jnp.concatenate([c_out.astype(BF16), att.astype(BF16)], axis=-1)
    h = x + jnp.dot(mix, w_out_ref[...], preferred_element_type=F32)
    hid = _mm(_rmsnorm(h, g_ffn_ref[...]), w_ff1_ref[...])
    hid = jnp.square(jnp.maximum(hid, 0.0))
    h = h + _mm(hid, w_ff2_ref[...])
    gate = jax.nn.sigmoid(_mm(_rmsnorm(h, g_ple_ref[...]), w_gate_ref[...]))
    h = h + gate * _mm(p, w_proj_ref[...])
    return _rmsnorm(h, g_fin_ref[...])


def _prompt_kernel(sink_ref, x_ref, p_ref, cos_ref, sin_ref,
                   g_mix_ref, w_in_ref, conv_w_ref, conv_b_ref, cn_g_ref, cn_b_ref, w_out_ref,
                   g_ffn_ref, w_ff1_ref, w_ff2_ref, g_ple_ref, w_gate_ref, w_proj_ref, g_fin_ref,
                   y_ref, nk_ref, nv_ref, nc_ref,
                   u_scr, k_scr, v_scr, att_scr, dc_scr):
    t = pl.program_id(1)

    @pl.when(t == 0)
    def _():
        u_scr[0:HIST, :] = jnp.zeros((HIST, CONV_CH), F32)
        k_scr[0:WINDOW, :] = jnp.zeros((WINDOW, KV_COLS), BF16)
        v_scr[0:WINDOW, :] = jnp.zeros((WINDOW, KV_COLS), BF16)

    x = x_ref[...]
    u, q, k, v = _in_proj(x, g_mix_ref, w_in_ref, cos_ref, sin_ref)

    u_scr[HIST:HIST + TILE, :] = u
    _depthwise_conv(u_scr, 0, TILE, 128, conv_w_ref, conv_b_ref, dc_scr, 0)
    nc_ref[...] = u_scr[TILE + HIST_PAD:TILE + HIST, :]
    u_scr[0:HIST, :] = u_scr[TILE:TILE + HIST, :]
    c_out = jax.nn.silu(_layernorm(dc_scr[...], cn_g_ref[...], cn_b_ref[...]))

    k_scr[WINDOW:WINDOW + TILE, :] = k.astype(BF16)
    v_scr[WINDOW:WINDOW + TILE, :] = v.astype(BF16)
    nk_ref[...] = k[TILE - WINDOW:, :]
    nv_ref[...] = v[TILE - WINDOW:, :]
    n_keys = WINDOW + CHUNK
    for c in range(TILE // CHUNK):
        r0 = c * CHUNK
        for kh in range(N_KV_HEADS):
            hs = slice(kh * HEAD_DIM, (kh + 1) * HEAD_DIM)
            qs = _stack_heads(q, r0, CHUNK, kh)
            s = lax.dot_general(qs, k_scr[r0:r0 + n_keys, hs], (((1,), (1,)), ((), ())),
                                preferred_element_type=F32)
            if c < WIN_CHUNKS:
                first_valid = jnp.maximum(WIN_CHUNKS - (t * (TILE // CHUNK) + c), 0) * CHUNK
                key_idx = lax.broadcasted_iota(jnp.int32, s.shape, 1)
                s = jnp.where(key_idx >= first_valid, s, NEG)
            sinks = [sink_ref[kh * GROUP + g] for g in range(GROUP)]
            o = _softmax_pv(s, sinks, v_scr[r0:r0 + n_keys, hs])
            for g in range(GROUP):
                h = kh * GROUP + g
                att_scr[r0:r0 + CHUNK, h * HEAD_DIM:(h + 1) * HEAD_DIM] = o[g * CHUNK:(g + 1) * CHUNK]
    k_scr[0:WINDOW, :] = k_scr[TILE:TILE + WINDOW, :]
    v_scr[0:WINDOW, :] = v_scr[TILE:TILE + WINDOW, :]

    y_ref[...] = _post_mix(x, c_out, att_scr[...], p_ref[...], w_out_ref, g_ffn_ref, w_ff1_ref,
                           w_ff2_ref, g_ple_ref, w_gate_ref, w_proj_ref, g_fin_ref)


def _sample_kernel(n_batch, n_tok,
                   sink_ref, x_ref, p_ref, cos_ref, sin_ref, ck_ref, cv_ref, state_ref,
                   g_mix_ref, w_in_ref, conv_w_ref, conv_b_ref, cn_g_ref, cn_b_ref, w_out_ref,
                   g_ffn_ref, w_ff1_ref, w_ff2_ref, g_ple_ref, w_gate_ref, w_proj_ref, g_fin_ref,
                   y_ref, nk_ref, nv_ref, nc_ref,
                   u_scr, att_scr, dc_scr):
    x = x_ref[...]
    u, q, k, v = _in_proj(x, g_mix_ref, w_in_ref, cos_ref, sin_ref)
    nk_ref[...] = k
    nv_ref[...] = v

    stride = HIST + n_tok
    for b in range(n_batch):
        base = b * stride
        u_scr[base:base + HIST_PAD, :] = jnp.zeros((HIST_PAD, CONV_CH), F32)
        u_scr[base + HIST_PAD:base + HIST, :] = state_ref[b]
        u_scr[base + HIST:base + stride, :] = u[b * n_tok:(b + 1) * n_tok]
        _depthwise_conv(u_scr, base, n_tok, n_tok, conv_w_ref, conv_b_ref, dc_scr, b * n_tok)
        nc_ref[b] = u_scr[base + stride - (CONV_WIDTH - 1):base + stride, :]
    c_out = jax.nn.silu(_layernorm(dc_scr[...], cn_g_ref[...], cn_b_ref[...]))

    kb = k.astype(BF16)
    vb = v.astype(BF16)
    for b in range(n_batch):
        r0 = b * n_tok
        for kh in range(N_KV_HEADS):
            hs = slice(kh * HEAD_DIM, (kh + 1) * HEAD_DIM)
            k_win = jnp.concatenate([ck_ref[b][:, hs].astype(BF16), kb[r0:r0 + n_tok, hs]], axis=0)
            v_win = jnp.concatenate([cv_ref[b][:, hs].astype(BF16), vb[r0:r0 + n_tok, hs]], axis=0)
            qs = _stack_heads(q, r0, n_tok, kh)
            s = lax.dot_general(qs, k_win, (((1,), (1,)), ((), ())), preferred_element_type=F32)
            sinks = [sink_ref[kh * GROUP + g] for g in range(GROUP)]
            o = _softmax_pv(s, sinks, v_win)
            for g in range(GROUP):
                h = kh * GROUP + g
                att_scr[r0:r0 + n_tok, h * HEAD_DIM:(h + 1) * HEAD_DIM] = o[g * n_tok:(g + 1) * n_tok]

    y_ref[...] = _post_mix(x, c_out, att_scr[...], p_ref[...], w_out_ref, g_ffn_ref, w_ff1_ref,
                           w_ff2_ref, g_ple_ref, w_gate_ref, w_proj_ref, g_fin_ref)


def _rope_tables(pos):
    half = HEAD_DIM // 2
    inv = 1.0 / (ROPE_THETA ** (jnp.arange(half, dtype=F32) / half))
    ang = pos[:, None] * inv[None, :]
    cos = jnp.cos(ang)
    sin = jnp.sin(ang)
    reps = LANES // HEAD_DIM
    return (jnp.tile(jnp.concatenate([cos, cos], axis=-1), (1, reps)),
            jnp.tile(jnp.concatenate([-sin, sin], axis=-1), (1, reps)))


def _resident(shape):
    nd = len(shape)
    return pl.BlockSpec(shape, lambda *_: (0,) * nd, pipeline_mode=pl.Buffered(1))


def _weight_specs():
    shapes = [(1, D_MODEL), (D_MODEL, IN_COLS), (CONV_WIDTH, CONV_CH), (1, CONV_CH), (1, CONV_CH),
              (1, CONV_CH), (D_MODEL, D_MODEL), (1, D_MODEL), (D_MODEL, D_FF), (D_FF, D_MODEL),
              (1, D_MODEL), (D_MODEL, D_MODEL), (D_PLE, D_MODEL), (1, D_MODEL)]
    return [_resident(s) for s in shapes]


def kernel(x_prompt, x_sample, p_prompt, p_sample, cache_k, cache_v, state_conv, ln_mix_g, w_in, conv_w, conv_b, conv_norm_g, conv_norm_b, attn_sink, w_out, ln_ffn_g, w_ff1, w_ff2, ple_norm_g, w_ple_gate, w_ple_proj, final_norm_g):
    assert ln_mix_g.shape[0] == 1, "single-layer kernel"
    n_b, seq, _ = x_prompt.shape
    n_db, n_tok, _ = x_sample.shape
    assert seq % TILE == 0 and TILE % CHUNK == 0 and TILE >= WINDOW and TILE >= HIST

    row = lambda a: a.reshape(1, -1).astype(F32)
    weights = (row(ln_mix_g[0]), w_in[0].astype(BF16), conv_w[0], row(conv_b[0]),
               row(conv_norm_g[0]), row(conv_norm_b[0]), w_out[0].astype(BF16), row(ln_ffn_g[0]),
               w_ff1[0].astype(BF16), w_ff2[0].astype(BF16), row(ple_norm_g[0]),
               w_ple_gate[0].astype(BF16), w_ple_proj[0].astype(BF16), row(final_norm_g))
    sink = attn_sink[0].astype(F32)
    smem = pl.BlockSpec(memory_space=pltpu.SMEM)

    cos_p, sin_p = _rope_tables(jnp.arange(seq, dtype=F32))
    tok = lambda width: pl.BlockSpec((None, TILE, width), lambda b, t: (b, t, 0))
    tab = pl.BlockSpec((TILE, LANES), lambda b, t: (t, 0))
    per_batch = lambda rows, width: pl.BlockSpec((None, rows, width), lambda b, t: (b, 0, 0))
    y_p, nk_p, nv_p, nc_p = pl.pallas_call(
        _prompt_kernel,
        grid=(n_b, seq // TILE),
        in_specs=[smem, tok(D_MODEL), tok(D_PLE), tab, tab] + _weight_specs(),
        out_specs=[tok(D_MODEL), per_batch(WINDOW, KV_COLS), per_batch(WINDOW, KV_COLS),
                   per_batch(CONV_WIDTH - 1, CONV_CH)],
        out_shape=[jax.ShapeDtypeStruct((n_b, seq, D_MODEL), F32),
                   jax.ShapeDtypeStruct((n_b, WINDOW, KV_COLS), F32),
                   jax.ShapeDtypeStruct((n_b, WINDOW, KV_COLS), F32),
                   jax.ShapeDtypeStruct((n_b, CONV_WIDTH - 1, CONV_CH), F32)],
        scratch_shapes=[pltpu.VMEM((HIST + TILE, CONV_CH), F32),
                        pltpu.VMEM((WINDOW + TILE, KV_COLS), BF16),
                        pltpu.VMEM((WINDOW + TILE, KV_COLS), BF16),
                        pltpu.VMEM((TILE, Q_COLS), F32),
                        pltpu.VMEM((TILE, CONV_CH), F32)],
        compiler_params=pltpu.CompilerParams(dimension_semantics=("arbitrary", "arbitrary"),
                                             vmem_limit_bytes=VMEM_LIMIT_BYTES),
        name="prompt_layer",
    )(sink, x_prompt, p_prompt[0], cos_p, sin_p, *weights)

    rows = n_db * n_tok
    cos_s, sin_s = _rope_tables(PAST_LEN + jnp.arange(n_tok, dtype=F32))
    full = lambda shape: pl.BlockSpec(shape, lambda i: (0,) * len(shape))
    y_s, nk_s, nv_s, nc_s = pl.pallas_call(
        functools.partial(_sample_kernel, n_db, n_tok),
        grid=(1,),
        in_specs=[smem, full((rows, D_MODEL)), full((rows, D_PLE)), full((rows, LANES)),
                  full((rows, LANES)), full((n_db, WINDOW, KV_COLS)), full((n_db, WINDOW, KV_COLS)),
                  full((n_db, CONV_WIDTH - 1, CONV_CH))] + _weight_specs(),
        out_specs=[full((rows, D_MODEL)), full((rows, KV_COLS)), full((rows, KV_COLS)),
                   full((n_db, CONV_WIDTH - 1, CONV_CH))],
        out_shape=[jax.ShapeDtypeStruct((rows, D_MODEL), F32),
                   jax.ShapeDtypeStruct((rows, KV_COLS), F32),
                   jax.ShapeDtypeStruct((rows, KV_COLS), F32),
                   jax.ShapeDtypeStruct((n_db, CONV_WIDTH - 1, CONV_CH), F32)],
        scratch_shapes=[pltpu.VMEM((n_db * (HIST + n_tok), CONV_CH), F32),
                        pltpu.VMEM((rows, Q_COLS), F32),
                        pltpu.VMEM((rows, CONV_CH), F32)],
        compiler_params=pltpu.CompilerParams(dimension_semantics=("arbitrary",),
                                             vmem_limit_bytes=VMEM_LIMIT_BYTES),
        name="sample_layer",
    )(sink, x_sample.reshape(rows, D_MODEL), p_sample[0].reshape(rows, D_PLE),
      jnp.tile(cos_s, (n_db, 1)), jnp.tile(sin_s, (n_db, 1)),
      cache_k[0].reshape(n_db, WINDOW, KV_COLS), cache_v[0].reshape(n_db, WINDOW, KV_COLS),
      state_conv[0], *weights)

    kv5 = lambda a, n, t: a.reshape(1, n, t, N_KV_HEADS, HEAD_DIM)
    return (y_p, y_s.reshape(n_db, n_tok, D_MODEL),
            kv5(nk_p, n_b, WINDOW), kv5(nv_p, n_b, WINDOW), nc_p[None],
            kv5(nk_s, n_db, n_tok), kv5(nv_s, n_db, n_tok), nc_s[None])
```

```python
import functools

import jax
import jax.numpy as jnp
from jax import lax
from jax.experimental import pallas as pl
from jax.experimental.pallas import tpu as pltpu

D_MODEL = 1024
CONV_CH = 512
CONV_WIDTH = 31
HEAD_DIM = 64
N_HEADS = 8
N_KV_HEADS = 2
GROUP = N_HEADS // N_KV_HEADS
CHUNK = 64
WINDOW = 128
WIN_CHUNKS = WINDOW // CHUNK
ROPE_THETA = 10000.0
D_FF = 4 * D_MODEL
D_PLE = 256
EPS = 1e-6
NEG = -1e30
PAST_LEN = 2048
Q_COLS = N_HEADS * HEAD_DIM
KV_COLS = N_KV_HEADS * HEAD_DIM
IN_COLS = 2 * CONV_CH + Q_COLS + 2 * KV_COLS
MIX_COLS = CONV_CH + Q_COLS

LANES = 128
SUBLANES = 8
HIST = 32
HIST_PAD = HIST - (CONV_WIDTH - 1)
TILE = 256
VMEM_LIMIT_BYTES = 56 * 1024 * 1024

F32 = jnp.float32
BF16 = jnp.bfloat16


def _rmsnorm(x, g):
    return x * lax.rsqrt(jnp.mean(x * x, axis=-1, keepdims=True) + EPS) * g


def _layernorm(x, g, b):
    mu = jnp.mean(x, axis=-1, keepdims=True)
    xc = x - mu
    return xc * lax.rsqrt(jnp.mean(xc * xc, axis=-1, keepdims=True) + EPS) * g + b


def _mm(a, w):
    return jnp.dot(a.astype(BF16), w, preferred_element_type=F32)


def _rope(x, cos, sin_signed):
    lane = lax.broadcasted_iota(jnp.int32, x.shape, 1)
    first_half = (lane % HEAD_DIM) < (HEAD_DIM // 2)
    partner = jnp.where(first_half,
                        pltpu.roll(x, LANES - HEAD_DIM // 2, 1),
                        pltpu.roll(x, HEAD_DIM // 2, 1))
    return x * cos + partner * sin_signed


def _depthwise_conv(u_ref, base, nrows, row_block, w_ref, b_ref, out_ref, out_base):
    for r0 in range(0, nrows, row_block):
        for c0 in range(0, CONV_CH, LANES):
            cs = slice(c0, c0 + LANES)
            acc = jnp.broadcast_to(b_ref[:, cs], (row_block, LANES))
            for sub in range(SUBLANES):
                offs = [o for o in range(sub, HIST + 1, SUBLANES) if HIST_PAD <= o <= HIST]
                rows = row_block + (SUBLANES if sub else 0)
                part = None
                for o in offs:
                    j = o - HIST_PAD
                    term = w_ref[j:j + 1, cs] * u_ref[pl.ds(base + r0 + o - sub, rows), cs]
                    part = term if part is None else part + term
                acc = acc + part[sub:sub + row_block]
            out_ref[pl.ds(out_base + r0, row_block), cs] = acc


def _softmax_pv(s, sinks, v_win):
    rows = s.shape[0] // GROUP
    es, invs = [], []
    for g in range(GROUP):
        sg = s[g * rows:(g + 1) * rows]
        m = jnp.maximum(jnp.max(sg, axis=-1, keepdims=True), sinks[g])
        e = jnp.exp(sg - m)
        denom = jnp.sum(e, axis=-1, keepdims=True) + jnp.exp(sinks[g] - m)
        es.append(e.astype(BF16))
        invs.append(1.0 / denom)
    o = jnp.dot(jnp.concatenate(es, axis=0), v_win, preferred_element_type=F32)
    return [o[g * rows:(g + 1) * rows] * invs[g] for g in range(GROUP)]


def _stack_heads(q, r0, rows, kh):
    parts = []
    for g in range(GROUP):
        h = kh * GROUP + g
        parts.append(q[r0:r0 + rows, h * HEAD_DIM:(h + 1) * HEAD_DIM])
    return (jnp.concatenate(parts, axis=0) * (HEAD_DIM ** -0.5)).astype(BF16)


def _in_proj(x, g_mix_ref, w_in_ref, cos_ref, sin_ref):
    z = _mm(_rmsnorm(x, g_mix_ref[...]), w_in_ref[...])
    u = z[:, :CONV_CH] * jax.nn.sigmoid(z[:, CONV_CH:2 * CONV_CH])
    cos = cos_ref[...]
    sin = sin_ref[...]
    q0 = 2 * CONV_CH
    q = jnp.concatenate(
        [_rope(z[:, q0 + c:q0 + c + LANES], cos, sin) for c in range(0, Q_COLS, LANES)], axis=-1)
    k = _rope(z[:, q0 + Q_COLS:q0 + Q_COLS + KV_COLS], cos, sin)
    v = z[:, q0 + Q_COLS + KV_COLS:]
    return u, q, k, v


def _conv_out(dc_ref, cn_g_ref, cn_b_ref):
    return jax.nn.silu(_layernorm(dc_ref[...], cn_g_ref[...], cn_b_ref[...])).astype(BF16)


def _dense_half(x, mix, p, w_out_ref, g_ffn_ref, w_ff1_ref, w_ff2_ref, g_ple_ref,
                w_gate_ref, w_proj_ref, g_fin_ref):
    h = x + jnp.dot(mix, w_out_ref[...], preferred_element_type=F32)
    hid = _mm(_rmsnorm(h, g_ffn_ref[...]), w_ff1_ref[...])
    hid = jnp.square(jnp.maximum(hid, 0.0))
    h = h + _mm(hid, w_ff2_ref[...])
    gate = jax.nn.sigmoid(_mm(_rmsnorm(h, g_ple_ref[...]), w_gate_ref[...]))
    h = h + gate * _mm(p, w_proj_ref[...])
    return _rmsnorm(h, g_fin_ref[...])


def _store_heads(mix_ref, r0, rows, kh, blocks):
    for g, blk in enumerate(blocks):
        c0 = CONV_CH + (kh * GROUP + g) * HEAD_DIM
        mix_ref[r0:r0 + rows, c0:c0 + HEAD_DIM] = blk.astype(BF16)


def _prompt_kernel(tiles_per_seq, n_tiles,
                   sink_ref, x_ref, p_ref, cos_ref, sin_ref,
                   g_mix_ref, w_in_ref, conv_w_ref, conv_b_ref, cn_g_ref, cn_b_ref, w_out_ref,
                   g_ffn_ref, w_ff1_ref, w_ff2_ref, g_ple_ref, w_gate_ref, w_proj_ref, g_fin_ref,
                   y_ref, nk_ref, nv_ref, nc_ref,
                   u_scr, k_scr, v_scr, dc_scr, mix_scr, xprev_scr):
    s = pl.program_id(0)
    t = lax.rem(jnp.minimum(s, n_tiles - 1), tiles_per_seq)

    @pl.when(s == 0)
    def _():
        mix_scr[...] = jnp.zeros(mix_scr.shape, BF16)
        xprev_scr[...] = jnp.zeros(xprev_scr.shape, F32)

    @pl.when(t == 0)
    def _():
        u_scr[0:HIST, :] = jnp.zeros((HIST, CONV_CH), F32)
        k_scr[0:WINDOW, :] = jnp.zeros((WINDOW, KV_COLS), BF16)
        v_scr[0:WINDOW, :] = jnp.zeros((WINDOW, KV_COLS), BF16)

    y_ref[...] = _dense_half(xprev_scr[...], mix_scr[...], p_ref[...], w_out_ref, g_ffn_ref,
                             w_ff1_ref, w_ff2_ref, g_ple_ref, w_gate_ref, w_proj_ref, g_fin_ref)

    x = x_ref[...]
    u, q, k, v = _in_proj(x, g_mix_ref, w_in_ref, cos_ref, sin_ref)
    xprev_scr[...] = x

    u_scr[HIST:HIST + TILE, :] = u
    _depthwise_conv(u_scr, 0, TILE, 128, conv_w_ref, conv_b_ref, dc_scr, 0)
    nc_ref[...] = u_scr[TILE + HIST_PAD:TILE + HIST, :]
    u_scr[0:HIST, :] = u_scr[TILE:TILE + HIST, :]
    mix_scr[:, 0:CONV_CH] = _conv_out(dc_scr, cn_g_ref, cn_b_ref)

    k_scr[WINDOW:WINDOW + TILE, :] = k.astype(BF16)
    v_scr[WINDOW:WINDOW + TILE, :] = v.astype(BF16)
    nk_ref[...] = k[TILE - WINDOW:, :]
    nv_ref[...] = v[TILE - WINDOW:, :]
    n_keys = WINDOW + CHUNK
    for c in range(TILE // CHUNK):
        r0 = c * CHUNK
        for kh in range(N_KV_HEADS):
            hs = slice(kh * HEAD_DIM, (kh + 1) * HEAD_DIM)
            qs = _stack_heads(q, r0, CHUNK, kh)
            sc = lax.dot_general(qs, k_scr[r0:r0 + n_keys, hs], (((1,), (1,)), ((), ())),
                                 preferred_element_type=F32)
            if c < WIN_CHUNKS:
                first_valid = jnp.maximum(WIN_CHUNKS - (t * (TILE // CHUNK) + c), 0) * CHUNK
                key_idx = lax.broadcasted_iota(jnp.int32, sc.shape, 1)
                sc = jnp.where(key_idx >= first_valid, sc, NEG)
            sinks = [sink_ref[kh * GROUP + g] for g in range(GROUP)]
            _store_heads(mix_scr, r0, CHUNK, kh, _softmax_pv(sc, sinks, v_scr[r0:r0 + n_keys, hs]))
    k_scr[0:WINDOW, :] = k_scr[TILE:TILE + WINDOW, :]
    v_scr[0:WINDOW, :] = v_scr[TILE:TILE + WINDOW, :]


def _sample_kernel(n_batch, n_tok,
                   sink_ref, x_ref, p_ref, cos_ref, sin_ref, ck_ref, cv_ref, state_ref,
                   g_mix_ref, w_in_ref, conv_w_ref, conv_b_ref, cn_g_ref, cn_b_ref, w_out_ref,
                   g_ffn_ref, w_ff1_ref, w_ff2_ref, g_ple_ref, w_gate_ref, w_proj_ref, g_fin_ref,
                   y_ref, nk_ref, nv_ref, nc_ref,
                   u_scr, dc_scr, mix_scr):
    x = x_ref[...]
    u, q, k, v = _in_proj(x, g_mix_ref, w_in_ref, cos_ref, sin_ref)
    nk_ref[...] = k
    nv_ref[...] = v

    stride = HIST + n_tok
    for b in range(n_batch):
        base = b * stride
        u_scr[base:base + HIST_PAD, :] = jnp.zeros((HIST_PAD, CONV_CH), F32)
        u_scr[base + HIST_PAD:base + HIST, :] = state_ref[b]
        u_scr[base + HIST:base + stride, :] = u[b * n_tok:(b + 1) * n_tok]
        _depthwise_conv(u_scr, base, n_tok, n_tok, conv_w_ref, conv_b_ref, dc_scr, b * n_tok)
        nc_ref[b] = u_scr[base + stride - (CONV_WIDTH - 1):base + stride, :]
    mix_scr[:, 0:CONV_CH] = _conv_out(dc_scr, cn_g_ref, cn_b_ref)

    kb = k.astype(BF16)
    vb = v.astype(BF16)
    for b in range(n_batch):
        r0 = b * n_tok
        for kh in range(N_KV_HEADS):
            hs = slice(kh * HEAD_DIM, (kh + 1) * HEAD_DIM)
            k_win = jnp.concatenate([ck_ref[b][:, hs].astype(BF16), kb[r0:r0 + n_tok, hs]], axis=0)
            v_win = jnp.concatenate([cv_ref[b][:, hs].astype(BF16), vb[r0:r0 + n_tok, hs]], axis=0)
            qs = _stack_heads(q, r0, n_tok, kh)
            sc = lax.dot_general(qs, k_win, (((1,), (1,)), ((), ())), preferred_element_type=F32)
            sinks = [sink_ref[kh * GROUP + g] for g in range(GROUP)]
            _store_heads(mix_scr, r0, n_tok, kh, _softmax_pv(sc, sinks, v_win))

    y_ref[...] = _dense_half(x, mix_scr[...], p_ref[...], w_out_ref, g_ffn_ref, w_ff1_ref,
                             w_ff2_ref, g_ple_ref, w_gate_ref, w_proj_ref, g_fin_ref)


def _rope_tables(pos):
    half = HEAD_DIM // 2
    inv = 1.0 / (ROPE_THETA ** (jnp.arange(half, dtype=F32) / half))
    ang = pos[:, None] * inv[None, :]
    cos = jnp.cos(ang)
    sin = jnp.sin(ang)
    reps = LANES // HEAD_DIM
    return (jnp.tile(jnp.concatenate([cos, cos], axis=-1), (1, reps)),
            jnp.tile(jnp.concatenate([-sin, sin], axis=-1), (1, reps)))


def _resident(shape):
    nd = len(shape)
    return pl.BlockSpec(shape, lambda *_: (0,) * nd, pipeline_mode=pl.Buffered(1))


def _weight_specs():
    shapes = [(1, D_MODEL), (D_MODEL, IN_COLS), (CONV_WIDTH, CONV_CH), (1, CONV_CH), (1, CONV_CH),
              (1, CONV_CH), (MIX_COLS, D_MODEL), (1, D_MODEL), (D_MODEL, D_FF), (D_FF, D_MODEL),
              (1, D_MODEL), (D_MODEL, D_MODEL), (D_PLE, D_MODEL), (1, D_MODEL)]
    return [_resident(s) for s in shapes]


def kernel(x_prompt, x_sample, p_prompt, p_sample, cache_k, cache_v, state_conv, ln_mix_g, w_in, conv_w, conv_b, conv_norm_g, conv_norm_b, attn_sink, w_out, ln_ffn_g, w_ff1, w_ff2, ple_norm_g, w_ple_gate, w_ple_proj, final_norm_g):
    assert ln_mix_g.shape[0] == 1, "single-layer kernel"
    n_b, seq, _ = x_prompt.shape
    n_db, n_tok, _ = x_sample.shape
    assert seq % TILE == 0 and TILE % CHUNK == 0 and TILE >= WINDOW and TILE >= HIST

    row = lambda a: a.reshape(1, -1).astype(F32)
    weights = (row(ln_mix_g[0]), w_in[0].astype(BF16), conv_w[0], row(conv_b[0]),
               row(conv_norm_g[0]), row(conv_norm_b[0]), w_out[0].astype(BF16), row(ln_ffn_g[0]),
               w_ff1[0].astype(BF16), w_ff2[0].astype(BF16), row(ple_norm_g[0]),
               w_ple_gate[0].astype(BF16), w_ple_proj[0].astype(BF16), row(final_norm_g))
    sink = attn_sink[0].astype(F32)
    smem = pl.BlockSpec(memory_space=pltpu.SMEM)

    tps = seq // TILE
    n_tiles = n_b * tps
    mixer_tile = lambda s: jnp.minimum(s, n_tiles - 1)
    dense_tile = lambda s: jnp.maximum(s - 1, 0)
    cos_p, sin_p = _rope_tables(jnp.arange(seq, dtype=F32))
    tok = lambda width, tile: pl.BlockSpec(
        (None, TILE, width), lambda s: (tile(s) // tps, tile(s) % tps, 0))
    tab = pl.BlockSpec((TILE, LANES), lambda s: (mixer_tile(s) % tps, 0))
    per_seq = lambda rows, width: pl.BlockSpec(
        (None, rows, width), lambda s: (mixer_tile(s) // tps, 0, 0))
    y_p, nk_p, nv_p, nc_p = pl.pallas_call(
        functools.partial(_prompt_kernel, tps, n_tiles),
        grid=(n_tiles + 1,),
        in_specs=[smem, tok(D_MODEL, mixer_tile), tok(D_PLE, dense_tile), tab, tab] + _weight_specs(),
        out_specs=[tok(D_MODEL, dense_tile), per_seq(WINDOW, KV_COLS), per_seq(WINDOW, KV_COLS),
                   per_seq(CONV_WIDTH - 1, CONV_CH)],
        out_shape=[jax.ShapeDtypeStruct((n_b, seq, D_MODEL), F32),
                   jax.ShapeDtypeStruct((n_b, WINDOW, KV_COLS), F32),
                   jax.ShapeDtypeStruct((n_b, WINDOW, KV_COLS), F32),
                   jax.ShapeDtypeStruct((n_b, CONV_WIDTH - 1, CONV_CH), F32)],
        scratch_shapes=[pltpu.VMEM((HIST + TILE, CONV_CH), F32),
                        pltpu.VMEM((WINDOW + TILE, KV_COLS), BF16),
                        pltpu.VMEM((WINDOW + TILE, KV_COLS), BF16),
                        pltpu.VMEM((TILE, CONV_CH), F32),
                        pltpu.VMEM((TILE, MIX_COLS), BF16),
                        pltpu.VMEM((TILE, D_MODEL), F32)],
        compiler_params=pltpu.CompilerParams(dimension_semantics=("arbitrary",),
                                             vmem_limit_bytes=VMEM_LIMIT_BYTES),
        name="prompt_layer",
    )(sink, x_prompt, p_prompt[0], cos_p, sin_p, *weights)

    rows = n_db * n_tok
    cos_s, sin_s = _rope_tables(PAST_LEN + jnp.arange(n_tok, dtype=F32))
    full = lambda shape: pl.BlockSpec(shape, lambda i: (0,) * len(shape))
    y_s, nk_s, nv_s, nc_s = pl.pallas_call(
        functools.partial(_sample_kernel, n_db, n_tok),
        grid=(1,),
        in_specs=[smem, full((rows, D_MODEL)), full((rows, D_PLE)), full((rows, LANES)),
                  full((rows, LANES)), full((n_db, WINDOW, KV_COLS)), full((n_db, WINDOW, KV_COLS)),
                  full((n_db, CONV_WIDTH - 1, CONV_CH))] + _weight_specs(),
        out_specs=[full((rows, D_MODEL)), full((rows, KV_COLS)), full((rows, KV_COLS)),
                   full((n_db, CONV_WIDTH - 1, CONV_CH))],
        out_shape=[jax.ShapeDtypeStruct((rows, D_MODEL), F32),
                   jax.ShapeDtypeStruct((rows, KV_COLS), F32),
                   jax.ShapeDtypeStruct((rows, KV_COLS), F32),
                   jax.ShapeDtypeStruct((n_db, CONV_WIDTH - 1, CONV_CH), F32)],
        scratch_shapes=[pltpu.VMEM((n_db * (HIST + n_tok), CONV_CH), F32),
                        pltpu.VMEM((rows, CONV_CH), F32),
                        pltpu.VMEM((rows, MIX_COLS), BF16)],
        compiler_params=pltpu.CompilerParams(dimension_semantics=("arbitrary",),
                                             vmem_limit_bytes=VMEM_LIMIT_BYTES),
        name="sample_layer",
    )(sink, x_sample.reshape(rows, D_MODEL), p_sample[0].reshape(rows, D_PLE),
      jnp.tile(cos_s, (n_db, 1)), jnp.tile(sin_s, (n_db, 1)),
      cache_k[0].reshape(n_db, WINDOW, KV_COLS), cache_v[0].reshape(n_db, WINDOW, KV_COLS),
      state_conv[0], *weights)

    kv5 = lambda a, n, t: a.reshape(1, n, t, N_KV_HEADS, HEAD_DIM)
    return (y_p, y_s.reshape(n_db, n_tok, D_MODEL),
            kv5(nk_p, n_b, WINDOW), kv5(nv_p, n_b, WINDOW), nc_p[None],
            kv5(nk_s, n_db, n_tok), kv5(nv_s, n_db, n_tok), nc_s[None])
```

```python
import functools

import jax
import jax.numpy as jnp
import numpy as np
from jax import lax
from jax.experimental import pallas as pl
from jax.experimental.pallas import tpu as pltpu

D_MODEL = 1024
CONV_CH = 512
CONV_WIDTH = 31
HEAD_DIM = 64
N_HEADS = 8
N_KV_HEADS = 2
GROUP = N_HEADS // N_KV_HEADS
CHUNK = 64
WINDOW = 128
WIN_CHUNKS = WINDOW // CHUNK
ROPE_THETA = 10000.0
D_FF = 4 * D_MODEL
D_PLE = 256
EPS = 1e-6
NEG = -1e30
PAST_LEN = 2048
Q_COLS = N_HEADS * HEAD_DIM
KV_COLS = N_KV_HEADS * HEAD_DIM
IN_COLS = 2 * CONV_CH + Q_COLS + 2 * KV_COLS
MIX_COLS = CONV_CH + Q_COLS

LANES = 128
SUBLANES = 8
MXU_TILE = 256
HIST = 32
HIST_PAD = HIST - (CONV_WIDTH - 1)
TILE = 256
N_CHUNKS = TILE // CHUNK
N_KEYS = WINDOW + CHUNK
VMEM_LIMIT_BYTES = 56 * 1024 * 1024

F32 = jnp.float32
BF16 = jnp.bfloat16

assert 2 * HEAD_DIM == LANES and GROUP == 4


def _rmsnorm(x, g):
    return x * lax.rsqrt(jnp.mean(x * x, axis=-1, keepdims=True) + EPS) * g


def _layernorm(x, g, b):
    mu = jnp.mean(x, axis=-1, keepdims=True)
    xc = x - mu
    return xc * lax.rsqrt(jnp.mean(xc * xc, axis=-1, keepdims=True) + EPS) * g + b


def _mm(a, w):
    return jnp.dot(a.astype(BF16), w, preferred_element_type=F32)


def _rope(x, cos, sin_signed):
    lane = lax.broadcasted_iota(jnp.int32, x.shape, 1)
    first_half = (lane % HEAD_DIM) < (HEAD_DIM // 2)
    partner = jnp.where(first_half,
                        pltpu.roll(x, LANES - HEAD_DIM // 2, 1),
                        pltpu.roll(x, HEAD_DIM // 2, 1))
    return x * cos + partner * sin_signed


def _in_proj(x, g_mix_ref, w_in_ref):
    return _mm(_rmsnorm(x, g_mix_ref[...]), w_in_ref[...])


def _split_proj(z, rope_ref, u_ref, u_row0, q_ref):
    rows = q_ref.shape[0]
    u = z[:, :CONV_CH] * jax.nn.sigmoid(z[:, CONV_CH:2 * CONV_CH])
    u_ref[pl.ds(u_row0, rows), :] = u
    half = HEAD_DIM // 2
    cs = rope_ref[...]
    cs = jnp.concatenate([cs, cs], axis=-1)
    first_half = (lax.broadcasted_iota(jnp.int32, cs.shape, 1) % HEAD_DIM) < half
    cos = jnp.where(first_half, cs, pltpu.roll(cs, half, 1))
    sin = jnp.where(first_half, -pltpu.roll(cs, LANES - half, 1), cs)
    q0 = 2 * CONV_CH
    for c in range(0, Q_COLS, LANES):
        q = _rope(z[:, q0 + c:q0 + c + LANES], cos, sin)
        q_ref[:, c:c + LANES] = (q * (HEAD_DIM ** -0.5)).astype(q_ref.dtype)
    k = _rope(z[:, q0 + Q_COLS:q0 + Q_COLS + KV_COLS], cos, sin)
    v = z[:, q0 + Q_COLS + KV_COLS:IN_COLS]
    return k, v, u


def _depthwise_conv(u_ref, base, row_block, w_ref, b_ref):
    blocks = []
    for c0 in range(0, CONV_CH, LANES):
        cs = slice(c0, c0 + LANES)
        acc = jnp.broadcast_to(b_ref[:, cs], (row_block, LANES))
        for sub in range(SUBLANES):
            offs = [o for o in range(sub, HIST + 1, SUBLANES) if HIST_PAD <= o <= HIST]
            rows = row_block + (SUBLANES if sub else 0)
            part = None
            for o in offs:
                j = o - HIST_PAD
                term = w_ref[j:j + 1, cs] * u_ref[pl.ds(base + o - sub, rows), cs]
                part = term if part is None else part + term
            acc = acc + part[sub:sub + row_block]
        blocks.append(acc)
    return blocks


def _store_kv_pairs(kx_ref, vx_ref, parity, k, v):
    rows = k.shape[0]
    lo = lax.broadcasted_iota(jnp.int32, k.shape, 1) < HEAD_DIM

    def put(ref, kh, j, cols, val, tail=None):
        ref[parity, kh, j, WINDOW:WINDOW + rows, cols] = val
        ref[1 - parity, kh, j, 0:WINDOW, cols] = val[rows - WINDOW:] if tail is None else tail

    data = slice(0, LANES)
    for src, ref in ((k, kx_ref), (v, vx_ref)):
        swapped = pltpu.roll(src, HEAD_DIM, 1)
        for kh, (own_lo, own_hi) in enumerate(((src, swapped), (swapped, src))):
            put(ref, kh, 0, data, jnp.where(lo, own_lo, 0.0).astype(BF16))
            put(ref, kh, 1, data, jnp.where(lo, 0.0, own_hi).astype(BF16))
    ones = slice(LANES, 2 * LANES)
    lo_tail = lax.broadcasted_iota(jnp.int32, (WINDOW, LANES), 1) < HEAD_DIM
    for kh in range(N_KV_HEADS):
        put(vx_ref, kh, 0, ones, jnp.where(lo, 1.0, 0.0).astype(BF16),
            jnp.where(lo_tail, 1.0, 0.0).astype(BF16))
        put(vx_ref, kh, 1, ones, jnp.where(lo, 0.0, 1.0).astype(BF16),
            jnp.where(lo_tail, 0.0, 1.0).astype(BF16))


def _attend_pairs(q_ref, r0, kh, kx_ref, vx_ref, parity, sink_ref, first_valid, att_ref):
    a0 = kh * GROUP * HEAD_DIM
    q2 = jnp.concatenate([q_ref[pl.ds(r0, CHUNK), a0:a0 + LANES],
                          q_ref[pl.ds(r0, CHUNK), a0 + LANES:a0 + 2 * LANES]], axis=0)
    k_ext = jnp.concatenate([kx_ref[parity, kh, 0, pl.ds(r0, N_KEYS), :],
                             kx_ref[parity, kh, 1, pl.ds(r0, N_KEYS), :]], axis=0)
    v_ext = jnp.concatenate([vx_ref[parity, kh, 0, pl.ds(r0, N_KEYS), :],
                             vx_ref[parity, kh, 1, pl.ds(r0, N_KEYS), :]], axis=0)
    sc = lax.dot_general(q2, k_ext, (((1,), (1,)), ((), ())), preferred_element_type=F32)

    lane = lax.broadcasted_iota(jnp.int32, (CHUNK, LANES), 1)
    lo = lane < HEAD_DIM
    keys = (lane, jnp.where(lo, lane + LANES, lane - HEAD_DIM), lane + HEAD_DIM)
    es, ms = [], []
    for blk in range(2):
        rows = slice(blk * CHUNK, (blk + 1) * CHUNK)
        s0, s1, s2 = (sc[rows, j * LANES:(j + 1) * LANES] for j in range(3))
        if first_valid is not None:
            s0, s1, s2 = (jnp.where(key >= first_valid, s, NEG) for key, s in zip(keys, (s0, s1, s2)))
        h0 = kh * GROUP + 2 * blk
        m0 = jnp.maximum(jnp.max(jnp.maximum(s0, jnp.where(lo, s1, NEG)), axis=-1, keepdims=True),
                         sink_ref[h0])
        m1 = jnp.maximum(jnp.max(jnp.maximum(s2, jnp.where(lo, NEG, s1)), axis=-1, keepdims=True),
                         sink_ref[h0 + 1])
        e = jnp.concatenate([jnp.exp(s0 - m0), jnp.exp(s1 - jnp.where(lo, m0, m1)),
                             jnp.exp(s2 - m1)], axis=-1)
        es.append(e.astype(BF16))
        ms.append((m0, m1))
    o = jnp.dot(jnp.concatenate(es, axis=0), v_ext, preferred_element_type=F32)
    for blk in range(2):
        rows = slice(blk * CHUNK, (blk + 1) * CHUNK)
        h0 = kh * GROUP + 2 * blk
        m0, m1 = ms[blk]
        sink_term = jnp.where(lo, jnp.exp(sink_ref[h0] - m0), jnp.exp(sink_ref[h0 + 1] - m1))
        out = o[rows, 0:LANES] / (o[rows, LANES:2 * LANES] + sink_term)
        att_ref[pl.ds(r0, CHUNK), h0 * HEAD_DIM:h0 * HEAD_DIM + LANES] = out.astype(BF16)


def _attend(q_ref, r0, rows, kh, k_win, v_win, sink_ref, att_ref):
    heads = [kh * GROUP + g for g in range(GROUP)]
    qs = jnp.concatenate(
        [q_ref[pl.ds(r0, rows), h * HEAD_DIM:(h + 1) * HEAD_DIM] for h in heads], axis=0)
    sc = lax.dot_general(qs.astype(BF16), k_win, (((1,), (1,)), ((), ())),
                         preferred_element_type=F32)
    es, invs = [], []
    for g, h in enumerate(heads):
        sg = sc[g * rows:(g + 1) * rows]
        sink = sink_ref[h]
        m = jnp.maximum(jnp.max(sg, axis=-1, keepdims=True), sink)
        e = jnp.exp(sg - m)
        denom = jnp.sum(e, axis=-1, keepdims=True) + jnp.exp(sink - m)
        es.append(e.astype(BF16))
        invs.append(1.0 / denom)
    o = jnp.dot(jnp.concatenate(es, axis=0), v_win, preferred_element_type=F32)
    for g, h in enumerate(heads):
        c0 = h * HEAD_DIM
        att_ref[pl.ds(r0, rows), c0:c0 + HEAD_DIM] = (
            o[g * rows:(g + 1) * rows] * invs[g]).astype(att_ref.dtype)


def _out_proj(x, dc, att, cn_g_ref, cn_b_ref, w_out_ref):
    c_out = jax.nn.silu(_layernorm(dc, cn_g_ref[...], cn_b_ref[...]))
    return (x + _mm(c_out, w_out_ref[0:CONV_CH, :])
            + _mm(att, w_out_ref[CONV_CH:MIX_COLS, :]))


def _issue_after(w, dep, row0, col0):
    rows = 2 * SUBLANES
    folded = dep[0:SUBLANES, 0:LANES]
    for r in range(SUBLANES, dep.shape[0], SUBLANES):
        folded = folded + dep[r:r + SUBLANES, 0:LANES]
    zero = jnp.minimum(jnp.abs(folded), 0.0)
    zero = jnp.concatenate([zero, zero], axis=0).astype(w.dtype)
    band = w[row0:row0 + rows]
    parts = [band[:, :col0]] if col0 else []
    parts.append(band[:, col0:col0 + LANES] + zero)
    if col0 + LANES < w.shape[1]:
        parts.append(band[:, col0 + LANES:])
    out = [w[:row0]] if row0 else []
    out.append(jnp.concatenate(parts, axis=1))
    if row0 + rows < w.shape[0]:
        out.append(w[row0 + rows:])
    return jnp.concatenate(out, axis=0)


def _ffn(h, f, w_ff1_ref, w_ff2_ref, beside=()):
    tiles_per_dot = (D_MODEL // MXU_TILE) * (D_FF // MXU_TILE)
    stride = max(1, (2 * tiles_per_dot - 8) // max(1, len(beside)))
    ties = {}
    for i, dep in enumerate(beside):
        slot = 8 + stride * i
        ties.setdefault(slot // tiles_per_dot, []).append((slot % tiles_per_dot, dep))

    def tied(w_ref, dot_idx):
        w = w_ref[...]
        tiles_k = w.shape[0] // MXU_TILE
        for j, dep in ties.get(dot_idx, ()):
            w = _issue_after(w, dep, (j % tiles_k) * MXU_TILE, (j // tiles_k) * MXU_TILE)
        return w

    hid = jnp.dot(f, tied(w_ff1_ref, 0), preferred_element_type=F32)
    return h + _mm(jnp.square(jnp.maximum(hid, 0.0)), tied(w_ff2_ref, 1))


def _dense_post(h, p, g_ple_ref, w_gate_ref, w_proj_ref, g_fin_ref):
    gate = jax.nn.sigmoid(_mm(_rmsnorm(h, g_ple_ref[...]), w_gate_ref[...]))
    h = h + gate * _mm(p, w_proj_ref[...])
    return _rmsnorm(h, g_fin_ref[...])


def _prompt_kernel(tiles_per_seq, n_tiles,
                   sink_ref, x_ref, xm_ref, p_ref, rope_ref,
                   g_mix_ref, w_in_ref, conv_w_ref, conv_b_ref, cn_g_ref, cn_b_ref, w_out_ref,
                   g_ffn_ref, w_ff1_ref, w_ff2_ref, g_ple_ref, w_gate_ref, w_proj_ref, g_fin_ref,
                   y_ref, nk_ref, nv_ref, nc_ref,
                   u_scr, q_scr, kx_scr, vx_scr, att_scr, z_scr, f_scr, h_scr):
    s = pl.program_id(0)
    t = lax.rem(jnp.clip(s - 1, 0, n_tiles - 1), tiles_per_seq)
    parity = lax.rem(s, 2)

    @pl.when(s == 0)
    def _():
        z_scr[...] = jnp.zeros(z_scr.shape, F32)
        h_scr[...] = jnp.zeros(h_scr.shape, F32)
        f_scr[...] = jnp.zeros(f_scr.shape, BF16)

    @pl.when(t == 0)
    def _():
        u_scr[parity, 0:HIST, :] = jnp.zeros((HIST, CONV_CH), F32)
        kx_scr[parity, :, :, 0:WINDOW, :] = jnp.zeros((N_KV_HEADS, 2, WINDOW, LANES), BF16)
        vx_scr[parity, :, :, 0:WINDOW, :] = jnp.zeros((N_KV_HEADS, 2, WINDOW, 2 * LANES), BF16)

    h = h_scr[...]
    f = f_scr[...]

    u_cur = u_scr.at[parity]
    k, v, u = _split_proj(z_scr, rope_ref, u_cur, HIST, q_scr)
    u_scr[1 - parity, 0:HIST, :] = u[TILE - HIST:, :]
    _store_kv_pairs(kx_scr, vx_scr, parity, k, v)
    nk_ref[...] = k[TILE - WINDOW:, :]
    nv_ref[...] = v[TILE - WINDOW:, :]
    nc_ref[...] = u[TILE - (CONV_WIDTH - 1):, :]
    conv_rows = [_depthwise_conv(u_cur, r0, CHUNK, conv_w_ref, conv_b_ref)
                 for r0 in range(0, TILE, CHUNK)]
    dc = jnp.concatenate([jnp.concatenate(blocks, axis=-1) for blocks in conv_rows], axis=0)
    h = _ffn(h, f, w_ff1_ref, w_ff2_ref, beside=[blk for blocks in conv_rows for blk in blocks])
    for c in range(N_CHUNKS):
        first_valid = None
        if c < WIN_CHUNKS:
            first_valid = jnp.maximum(WIN_CHUNKS - (t * N_CHUNKS + c), 0) * CHUNK
        for kh in range(N_KV_HEADS):
            _attend_pairs(q_scr, c * CHUNK, kh, kx_scr, vx_scr, parity, sink_ref, first_valid,
                          att_scr)
    z_scr[...] = _in_proj(x_ref[...], g_mix_ref, w_in_ref)
    y_ref[...] = _dense_post(h, p_ref[...], g_ple_ref, w_gate_ref, w_proj_ref, g_fin_ref)
    h1 = _out_proj(xm_ref[...], dc, att_scr[...], cn_g_ref, cn_b_ref, w_out_ref)
    h_scr[...] = h1
    f_scr[...] = _rmsnorm(h1, g_ffn_ref[...]).astype(BF16)


def _sample_kernel(n_batch, n_tok,
                   sink_ref, x_ref, p_ref, rope_ref, ck_ref, cv_ref, state_ref,
                   g_mix_ref, w_in_ref, conv_w_ref, conv_b_ref, cn_g_ref, cn_b_ref, w_out_ref,
                   g_ffn_ref, w_ff1_ref, w_ff2_ref, g_ple_ref, w_gate_ref, w_proj_ref, g_fin_ref,
                   y_ref, nk_ref, nv_ref, nc_ref,
                   u_scr, us_scr, q_scr, att_scr):
    x = x_ref[...]
    k, v, _ = _split_proj(_in_proj(x, g_mix_ref, w_in_ref), rope_ref, u_scr, 0, q_scr)
    nk_ref[...] = k
    nv_ref[...] = v

    stride = HIST + n_tok
    dcs = []
    for b in range(n_batch):
        base = b * stride
        us_scr[base:base + HIST_PAD, :] = jnp.zeros((HIST_PAD, CONV_CH), F32)
        us_scr[base + HIST_PAD:base + HIST, :] = state_ref[b]
        us_scr[base + HIST:base + stride, :] = u_scr[b * n_tok:(b + 1) * n_tok, :]
        dcs.append(jnp.concatenate(
            _depthwise_conv(us_scr, base, n_tok, conv_w_ref, conv_b_ref), axis=-1))
        nc_ref[b] = us_scr[base + stride - (CONV_WIDTH - 1):base + stride, :]
    dc = jnp.concatenate(dcs, axis=0)

    kb = k.astype(BF16)
    vb = v.astype(BF16)
    for b in range(n_batch):
        r0 = b * n_tok
        for kh in range(N_KV_HEADS):
            hs = slice(kh * HEAD_DIM, (kh + 1) * HEAD_DIM)
            k_win = jnp.concatenate([ck_ref[b][:, hs].astype(BF16), kb[r0:r0 + n_tok, hs]], axis=0)
            v_win = jnp.concatenate([cv_ref[b][:, hs].astype(BF16), vb[r0:r0 + n_tok, hs]], axis=0)
            _attend(q_scr, r0, n_tok, kh, k_win, v_win, sink_ref, att_scr)

    h = _out_proj(x, dc, att_scr[...], cn_g_ref, cn_b_ref, w_out_ref)
    h = _ffn(h, _rmsnorm(h, g_ffn_ref[...]).astype(BF16), w_ff1_ref, w_ff2_ref)
    y_ref[...] = _dense_post(h, p_ref[...], g_ple_ref, w_gate_ref, w_proj_ref, g_fin_ref)


def _rope_tables(first_pos, count, repeats=1):
    half = HEAD_DIM // 2
    inv = 1.0 / (ROPE_THETA ** (np.arange(half, dtype=np.float64) / half))
    ang = (first_pos + np.arange(count, dtype=np.float64))[:, None] * inv[None, :]
    table = np.concatenate([np.cos(ang), np.sin(ang)], axis=-1)
    return jnp.asarray(np.tile(table, (repeats, 1)), dtype=F32)


def _resident(shape):
    nd = len(shape)
    return pl.BlockSpec(shape, lambda *_: (0,) * nd, pipeline_mode=pl.Buffered(1))


def _weight_specs():
    shapes = [(1, D_MODEL), (D_MODEL, IN_COLS), (CONV_WIDTH, CONV_CH), (1, CONV_CH), (1, CONV_CH),
              (1, CONV_CH), (MIX_COLS, D_MODEL), (1, D_MODEL), (D_MODEL, D_FF),
              (D_FF, D_MODEL), (1, D_MODEL), (D_MODEL, D_MODEL), (D_PLE, D_MODEL),
              (1, D_MODEL)]
    return [_resident(s) for s in shapes]


def kernel(x_prompt, x_sample, p_prompt, p_sample, cache_k, cache_v, state_conv, ln_mix_g, w_in, conv_w, conv_b, conv_norm_g, conv_norm_b, attn_sink, w_out, ln_ffn_g, w_ff1, w_ff2, ple_norm_g, w_ple_gate, w_ple_proj, final_norm_g):
    assert ln_mix_g.shape[0] == 1, "single-layer kernel"
    n_b, seq, _ = x_prompt.shape
    n_db, n_tok, _ = x_sample.shape
    assert seq % TILE == 0 and TILE % CHUNK == 0 and TILE >= WINDOW and TILE >= HIST

    row = lambda a: a.reshape(1, -1).astype(F32)
    weights = (row(ln_mix_g[0]), w_in[0].astype(BF16), conv_w[0], row(conv_b[0]),
               row(conv_norm_g[0]), row(conv_norm_b[0]), w_out[0].astype(BF16), row(ln_ffn_g[0]),
               w_ff1[0].astype(BF16), w_ff2[0].astype(BF16), row(ple_norm_g[0]),
               w_ple_gate[0].astype(BF16), w_ple_proj[0].astype(BF16), row(final_norm_g))
    sink = attn_sink[0].astype(F32)
    smem = pl.BlockSpec(memory_space=pltpu.SMEM)

    tps = seq // TILE
    n_tiles = n_b * tps
    proj_tile = lambda s: jnp.minimum(s, n_tiles - 1)
    mixer_tile = lambda s: jnp.clip(s - 1, 0, n_tiles - 1)
    dense_tile = lambda s: jnp.maximum(s - 2, 0)
    rope_p = _rope_tables(0, seq)
    tok = lambda width, tile: pl.BlockSpec(
        (None, TILE, width), lambda s: (tile(s) // tps, tile(s) % tps, 0))
    tab = pl.BlockSpec((TILE, HEAD_DIM), lambda s: (mixer_tile(s) % tps, 0))
    per_seq = lambda rows, width: pl.BlockSpec(
        (None, rows, width), lambda s: (mixer_tile(s) // tps, 0, 0))
    y_p, nk_p, nv_p, nc_p = pl.pallas_call(
        functools.partial(_prompt_kernel, tps, n_tiles),
        grid=(n_tiles + 2,),
        in_specs=[smem, tok(D_MODEL, proj_tile), tok(D_MODEL, mixer_tile), tok(D_PLE, dense_tile),
                  tab] + _weight_specs(),
        out_specs=[tok(D_MODEL, dense_tile), per_seq(WINDOW, KV_COLS), per_seq(WINDOW, KV_COLS),
                   per_seq(CONV_WIDTH - 1, CONV_CH)],
        out_shape=[jax.ShapeDtypeStruct((n_b, seq, D_MODEL), F32),
                   jax.ShapeDtypeStruct((n_b, WINDOW, KV_COLS), F32),
                   jax.ShapeDtypeStruct((n_b, WINDOW, KV_COLS), F32),
                   jax.ShapeDtypeStruct((n_b, CONV_WIDTH - 1, CONV_CH), F32)],
        scratch_shapes=[pltpu.VMEM((2, HIST + TILE, CONV_CH), F32),
                        pltpu.VMEM((TILE, Q_COLS), BF16),
                        pltpu.VMEM((2, N_KV_HEADS, 2, WINDOW + TILE, LANES), BF16),
                        pltpu.VMEM((2, N_KV_HEADS, 2, WINDOW + TILE, 2 * LANES), BF16),
                        pltpu.VMEM((TILE, Q_COLS), BF16),
                        pltpu.VMEM((TILE, IN_COLS), F32),
                        pltpu.VMEM((TILE, D_MODEL), BF16),
                        pltpu.VMEM((TILE, D_MODEL), F32)],
        compiler_params=pltpu.CompilerParams(
            dimension_semantics=("arbitrary",), vmem_limit_bytes=VMEM_LIMIT_BYTES),
        name="prompt_layer",
    )(sink, x_prompt, x_prompt, p_prompt[0], rope_p, *weights)

    rows = n_db * n_tok
    rope_s = _rope_tables(PAST_LEN, n_tok, repeats=n_db)
    full = lambda shape: pl.BlockSpec(shape, lambda i: (0,) * len(shape))
    y_s, nk_s, nv_s, nc_s = pl.pallas_call(
        functools.partial(_sample_kernel, n_db, n_tok),
        grid=(1,),
        in_specs=[smem, full((rows, D_MODEL)), full((rows, D_PLE)), full((rows, HEAD_DIM)),
                  full((n_db, WINDOW, KV_COLS)), full((n_db, WINDOW, KV_COLS)),
                  full((n_db, CONV_WIDTH - 1, CONV_CH))] + _weight_specs(),
        out_specs=[full((rows, D_MODEL)), full((rows, KV_COLS)), full((rows, KV_COLS)),
                   full((n_db, CONV_WIDTH - 1, CONV_CH))],
        out_shape=[jax.ShapeDtypeStruct((rows, D_MODEL), F32),
                   jax.ShapeDtypeStruct((rows, KV_COLS), F32),
                   jax.ShapeDtypeStruct((rows, KV_COLS), F32),
                   jax.ShapeDtypeStruct((n_db, CONV_WIDTH - 1, CONV_CH), F32)],
        scratch_shapes=[pltpu.VMEM((rows, CONV_CH), F32),
                        pltpu.VMEM((n_db * (HIST + n_tok), CONV_CH), F32),
                        pltpu.VMEM((rows, Q_COLS), F32),
                        pltpu.VMEM((rows, Q_COLS), F32)],
        compiler_params=pltpu.CompilerParams(dimension_semantics=("arbitrary",),
                                             vmem_limit_bytes=VMEM_LIMIT_BYTES),
        name="sample_layer",
    )(sink, x_sample.reshape(rows, D_MODEL), p_sample[0].reshape(rows, D_PLE),
      rope_s,
      cache_k[0].reshape(n_db, WINDOW, KV_COLS), cache_v[0].reshape(n_db, WINDOW, KV_COLS),
      state_conv[0], *weights)

    kv5 = lambda a, n, t: a.reshape(1, n, t, N_KV_HEADS, HEAD_DIM)
    return (y_p, y_s.reshape(n_db, n_tok, D_MODEL),
            kv5(nk_p, n_b, WINDOW), kv5(nv_p, n_b, WINDOW), nc_p[None],
            kv5(nk_s, n_db, n_tok), kv5(nv_s, n_db, n_tok), nc_s[None])
```

```python
import functools

import jax
import jax.numpy as jnp
import numpy as np
from jax import lax
from jax.experimental import pallas as pl
from jax.experimental.pallas import tpu as pltpu

D_MODEL = 1024
CONV_CH = 512
CONV_WIDTH = 31
HEAD_DIM = 64
N_HEADS = 8
N_KV_HEADS = 2
GROUP = N_HEADS // N_KV_HEADS
CHUNK = 64
WINDOW = 128
WIN_CHUNKS = WINDOW // CHUNK
ROPE_THETA = 10000.0
D_FF = 4 * D_MODEL
D_PLE = 256
EPS = 1e-6
NEG = -1e30
PAST_LEN = 2048
Q_COLS = N_HEADS * HEAD_DIM
KV_COLS = N_KV_HEADS * HEAD_DIM
IN_COLS = 2 * CONV_CH + Q_COLS + 2 * KV_COLS
MIX_COLS = CONV_CH + Q_COLS

LANES = 128
SUBLANES = 8
MXU_TILE = 256
HIST = 32
HIST_PAD = HIST - (CONV_WIDTH - 1)
TILE = 256
N_CHUNKS = TILE // CHUNK
N_KEYS = WINDOW + CHUNK
VMEM_LIMIT_BYTES = 56 * 1024 * 1024

F32 = jnp.float32
BF16 = jnp.bfloat16

assert 2 * HEAD_DIM == LANES and GROUP == 4


def _rmsnorm(x, g):
    return x * lax.rsqrt(jnp.mean(x * x, axis=-1, keepdims=True) + EPS) * g


def _layernorm(x, g, b):
    mu = jnp.mean(x, axis=-1, keepdims=True)
    xc = x - mu
    return xc * lax.rsqrt(jnp.mean(xc * xc, axis=-1, keepdims=True) + EPS) * g + b


def _mm(a, w):
    return jnp.dot(a.astype(BF16), w, preferred_element_type=F32)


def _rope(x, cos, sin_signed):
    lane = lax.broadcasted_iota(jnp.int32, x.shape, 1)
    first_half = (lane % HEAD_DIM) < (HEAD_DIM // 2)
    partner = jnp.where(first_half,
                        pltpu.roll(x, LANES - HEAD_DIM // 2, 1),
                        pltpu.roll(x, HEAD_DIM // 2, 1))
    return x * cos + partner * sin_signed


def _in_proj(x, g_mix_ref, w_in_ref):
    return _mm(_rmsnorm(x, g_mix_ref[...]), w_in_ref[...])


def _split_proj(z, rope_ref, u_ref, u_row0, q_ref):
    rows = q_ref.shape[0]
    u = z[:, :CONV_CH] * jax.nn.sigmoid(z[:, CONV_CH:2 * CONV_CH])
    u_ref[pl.ds(u_row0, rows), :] = u
    half = HEAD_DIM // 2
    cs = rope_ref[...]
    cs = jnp.concatenate([cs, cs], axis=-1)
    first_half = (lax.broadcasted_iota(jnp.int32, cs.shape, 1) % HEAD_DIM) < half
    cos = jnp.where(first_half, cs, pltpu.roll(cs, half, 1))
    sin = jnp.where(first_half, -pltpu.roll(cs, LANES - half, 1), cs)
    q0 = 2 * CONV_CH
    for c in range(0, Q_COLS, LANES):
        q = _rope(z[:, q0 + c:q0 + c + LANES], cos, sin)
        q_ref[:, c:c + LANES] = (q * (HEAD_DIM ** -0.5)).astype(q_ref.dtype)
    k = _rope(z[:, q0 + Q_COLS:q0 + Q_COLS + KV_COLS], cos, sin)
    v = z[:, q0 + Q_COLS + KV_COLS:IN_COLS]
    return k, v, u


def _depthwise_conv(u_ref, base, row_block, w_ref, b_ref):
    blocks = []
    for c0 in range(0, CONV_CH, LANES):
        cs = slice(c0, c0 + LANES)
        acc = jnp.broadcast_to(b_ref[:, cs], (row_block, LANES))
        for sub in range(SUBLANES):
            offs = [o for o in range(sub, HIST + 1, SUBLANES) if HIST_PAD <= o <= HIST]
            rows = row_block + (SUBLANES if sub else 0)
            part = None
            for o in offs:
                j = o - HIST_PAD
                term = w_ref[j:j + 1, cs] * u_ref[pl.ds(base + o - sub, rows), cs]
                part = term if part is None else part + term
            acc = acc + part[sub:sub + row_block]
        blocks.append(acc)
    return blocks


def _store_kv_pairs(kx_ref, vx_ref, parity, k, v):
    rows = k.shape[0]
    lo = lax.broadcasted_iota(jnp.int32, k.shape, 1) < HEAD_DIM

    def put(ref, kh, j, cols, val, tail=None):
        ref[parity, kh, j, WINDOW:WINDOW + rows, cols] = val
        ref[1 - parity, kh, j, 0:WINDOW, cols] = val[rows - WINDOW:] if tail is None else tail

    data = slice(0, LANES)
    for src, ref in ((k, kx_ref), (v, vx_ref)):
        swapped = pltpu.roll(src, HEAD_DIM, 1)
        for kh, (own_lo, own_hi) in enumerate(((src, swapped), (swapped, src))):
            put(ref, kh, 0, data, jnp.where(lo, own_lo, 0.0).astype(BF16))
            put(ref, kh, 1, data, jnp.where(lo, 0.0, own_hi).astype(BF16))
    ones = slice(LANES, 2 * LANES)
    lo_tail = lax.broadcasted_iota(jnp.int32, (WINDOW, LANES), 1) < HEAD_DIM
    for kh in range(N_KV_HEADS):
        put(vx_ref, kh, 0, ones, jnp.where(lo, 1.0, 0.0).astype(BF16),
            jnp.where(lo_tail, 1.0, 0.0).astype(BF16))
        put(vx_ref, kh, 1, ones, jnp.where(lo, 0.0, 1.0).astype(BF16),
            jnp.where(lo_tail, 0.0, 1.0).astype(BF16))


def _attend_pairs(q_ref, r0, kh, kx_ref, vx_ref, parity, sink_ref, first_valid, att_ref):
    a0 = kh * GROUP * HEAD_DIM
    q2 = jnp.concatenate([q_ref[pl.ds(r0, CHUNK), a0:a0 + LANES],
                          q_ref[pl.ds(r0, CHUNK), a0 + LANES:a0 + 2 * LANES]], axis=0)
    k_ext = jnp.concatenate([kx_ref[parity, kh, 0, pl.ds(r0, N_KEYS), :],
                             kx_ref[parity, kh, 1, pl.ds(r0, N_KEYS), :]], axis=0)
    v_ext = jnp.concatenate([vx_ref[parity, kh, 0, pl.ds(r0, N_KEYS), :],
                             vx_ref[parity, kh, 1, pl.ds(r0, N_KEYS), :]], axis=0)
    sc = lax.dot_general(q2, k_ext, (((1,), (1,)), ((), ())), preferred_element_type=F32)

    lane = lax.broadcasted_iota(jnp.int32, (CHUNK, LANES), 1)
    lo = lane < HEAD_DIM
    keys = (lane, jnp.where(lo, lane + LANES, lane - HEAD_DIM), lane + HEAD_DIM)
    es, ms = [], []
    for blk in range(2):
        rows = slice(blk * CHUNK, (blk + 1) * CHUNK)
        s0, s1, s2 = (sc[rows, j * LANES:(j + 1) * LANES] for j in range(3))
        if first_valid is not None:
            s0, s1, s2 = (jnp.where(key >= first_valid, s, NEG) for key, s in zip(keys, (s0, s1, s2)))
        h0 = kh * GROUP + 2 * blk
        m0 = jnp.maximum(jnp.max(jnp.maximum(s0, jnp.where(lo, s1, NEG)), axis=-1, keepdims=True),
                         sink_ref[h0])
        m1 = jnp.maximum(jnp.max(jnp.maximum(s2, jnp.where(lo, NEG, s1)), axis=-1, keepdims=True),
                         sink_ref[h0 + 1])
        e = jnp.concatenate([jnp.exp(s0 - m0), jnp.exp(s1 - jnp.where(lo, m0, m1)),
                             jnp.exp(s2 - m1)], axis=-1)
        es.append(e.astype(BF16))
        ms.append((m0, m1))
    o = jnp.dot(jnp.concatenate(es, axis=0), v_ext, preferred_element_type=F32)
    for blk in range(2):
        rows = slice(blk * CHUNK, (blk + 1) * CHUNK)
        h0 = kh * GROUP + 2 * blk
        m0, m1 = ms[blk]
        sink_term = jnp.where(lo, jnp.exp(sink_ref[h0] - m0), jnp.exp(sink_ref[h0 + 1] - m1))
        out = o[rows, 0:LANES] / (o[rows, LANES:2 * LANES] + sink_term)
        att_ref[pl.ds(r0, CHUNK), h0 * HEAD_DIM:h0 * HEAD_DIM + LANES] = out.astype(BF16)


def _attend(q_ref, r0, rows, kh, k_win, v_win, sink_ref, att_ref):
    heads = [kh * GROUP + g for g in range(GROUP)]
    qs = jnp.concatenate(
        [q_ref[pl.ds(r0, rows), h * HEAD_DIM:(h + 1) * HEAD_DIM] for h in heads], axis=0)
    sc = lax.dot_general(qs.astype(BF16), k_win, (((1,), (1,)), ((), ())),
                         preferred_element_type=F32)
    es, invs = [], []
    for g, h in enumerate(heads):
        sg = sc[g * rows:(g + 1) * rows]
        sink = sink_ref[h]
        m = jnp.maximum(jnp.max(sg, axis=-1, keepdims=True), sink)
        e = jnp.exp(sg - m)
        denom = jnp.sum(e, axis=-1, keepdims=True) + jnp.exp(sink - m)
        es.append(e.astype(BF16))
        invs.append(1.0 / denom)
    o = jnp.dot(jnp.concatenate(es, axis=0), v_win, preferred_element_type=F32)
    for g, h in enumerate(heads):
        c0 = h * HEAD_DIM
        att_ref[pl.ds(r0, rows), c0:c0 + HEAD_DIM] = (
            o[g * rows:(g + 1) * rows] * invs[g]).astype(att_ref.dtype)


def _out_proj(x, dc, att, cn_g_ref, cn_b_ref, w_out_ref):
    c_out = jax.nn.silu(_layernorm(dc, cn_g_ref[...], cn_b_ref[...]))
    return (x + _mm(c_out, w_out_ref[0:CONV_CH, 0:D_MODEL])
            + _mm(att, w_out_ref[CONV_CH:MIX_COLS, 0:D_MODEL]))


def _issue_after(w, dep, row0, col0):
    rows = 2 * SUBLANES
    folded = dep[0:SUBLANES, 0:LANES]
    for r in range(SUBLANES, dep.shape[0], SUBLANES):
        folded = folded + dep[r:r + SUBLANES, 0:LANES]
    zero = jnp.minimum(jnp.abs(folded), 0.0)
    zero = jnp.concatenate([zero, zero], axis=0).astype(w.dtype)
    band = w[row0:row0 + rows]
    parts = [band[:, :col0]] if col0 else []
    parts.append(band[:, col0:col0 + LANES] + zero)
    if col0 + LANES < w.shape[1]:
        parts.append(band[:, col0 + LANES:])
    out = [w[:row0]] if row0 else []
    out.append(jnp.concatenate(parts, axis=1))
    if row0 + rows < w.shape[0]:
        out.append(w[row0 + rows:])
    return jnp.concatenate(out, axis=0)


def _ffn(h, f, w_ff1_ref, w_ff2_ref, beside=()):
    tiles_per_dot = (D_MODEL // MXU_TILE) * (D_FF // MXU_TILE)
    stride = max(1, (2 * tiles_per_dot - 8) // max(1, len(beside)))
    ties = {}
    for i, dep in enumerate(beside):
        slot = 8 + stride * i
        ties.setdefault(slot // tiles_per_dot, []).append((slot % tiles_per_dot, dep))

    def tied(w_ref, dot_idx):
        w = w_ref[:, 0:w_ref.shape[1] - LANES]
        tiles_k = w.shape[0] // MXU_TILE
        for j, dep in ties.get(dot_idx, ()):
            w = _issue_after(w, dep, (j % tiles_k) * MXU_TILE, (j // tiles_k) * MXU_TILE)
        return w

    hid = jnp.dot(f, tied(w_ff1_ref, 0), preferred_element_type=F32)
    return h + _mm(jnp.square(jnp.maximum(hid, 0.0)), tied(w_ff2_ref, 1))


def _dense_post(h, p, g_ple_ref, w_gate_ref, w_proj_ref, g_fin_ref):
    gate = jax.nn.sigmoid(_mm(_rmsnorm(h, g_ple_ref[...]), w_gate_ref[:, 0:D_MODEL]))
    h = h + gate * _mm(p, w_proj_ref[:, 0:D_MODEL])
    return _rmsnorm(h, g_fin_ref[...])


def _prompt_kernel(tiles_per_seq, n_tiles,
                   sink_ref, x_ref, xm_ref, p_ref, rope_ref,
                   g_mix_ref, w_in_ref, conv_w_ref, conv_b_ref, cn_g_ref, cn_b_ref, w_out_ref,
                   g_ffn_ref, w_ff1_ref, w_ff2_ref, g_ple_ref, w_gate_ref, w_proj_ref, g_fin_ref,
                   y_ref, nk_ref, nv_ref, nc_ref,
                   u_scr, q_scr, kx_scr, vx_scr, att_scr, z_scr, f_scr, h_scr):
    s = pl.program_id(0)
    t = lax.rem(jnp.clip(s - 1, 0, n_tiles - 1), tiles_per_seq)
    parity = lax.rem(s, 2)

    @pl.when(s == 0)
    def _():
        z_scr[...] = jnp.zeros(z_scr.shape, F32)
        h_scr[...] = jnp.zeros(h_scr.shape, F32)
        f_scr[...] = jnp.zeros(f_scr.shape, BF16)

    @pl.when(t == 0)
    def _():
        u_scr[parity, 0:HIST, :] = jnp.zeros((HIST, CONV_CH), F32)
        kx_scr[parity, :, :, 0:WINDOW, :] = jnp.zeros((N_KV_HEADS, 2, WINDOW, LANES), BF16)
        vx_scr[parity, :, :, 0:WINDOW, :] = jnp.zeros((N_KV_HEADS, 2, WINDOW, 2 * LANES), BF16)

    h = h_scr[...]
    f = f_scr[...]

    u_cur = u_scr.at[parity]
    k, v, u = _split_proj(z_scr, rope_ref, u_cur, HIST, q_scr)
    u_scr[1 - parity, 0:HIST, :] = u[TILE - HIST:, :]
    _store_kv_pairs(kx_scr, vx_scr, parity, k, v)
    nk_ref[...] = k[TILE - WINDOW:, :]
    nv_ref[...] = v[TILE - WINDOW:, :]
    nc_ref[...] = u[TILE - (CONV_WIDTH - 1):, :]
    conv_rows = [_depthwise_conv(u_cur, r0, CHUNK, conv_w_ref, conv_b_ref)
                 for r0 in range(0, TILE, CHUNK)]
    dc = jnp.concatenate([jnp.concatenate(blocks, axis=-1) for blocks in conv_rows], axis=0)
    h = _ffn(h, f, w_ff1_ref, w_ff2_ref, beside=[blk for blocks in conv_rows for blk in blocks])
    for c in range(N_CHUNKS):
        first_valid = None
        if c < WIN_CHUNKS:
            first_valid = jnp.maximum(WIN_CHUNKS - (t * N_CHUNKS + c), 0) * CHUNK
        for kh in range(N_KV_HEADS):
            _attend_pairs(q_scr, c * CHUNK, kh, kx_scr, vx_scr, parity, sink_ref, first_valid,
                          att_scr)
    z_scr[...] = _in_proj(x_ref[...], g_mix_ref, w_in_ref)
    y_ref[...] = _dense_post(h, p_ref[...], g_ple_ref, w_gate_ref, w_proj_ref, g_fin_ref)
    h1 = _out_proj(xm_ref[...], dc, att_scr[...], cn_g_ref, cn_b_ref, w_out_ref)
    h_scr[...] = h1
    f_scr[...] = _rmsnorm(h1, g_ffn_ref[...]).astype(BF16)


def _sample_kernel(n_batch, n_tok,
                   sink_ref, x_ref, p_ref, rope_ref, ck_ref, cv_ref, state_ref,
                   g_mix_ref, w_in_ref, conv_w_ref, conv_b_ref, cn_g_ref, cn_b_ref, w_out_ref,
                   g_ffn_ref, w_ff1_ref, w_ff2_ref, g_ple_ref, w_gate_ref, w_proj_ref, g_fin_ref,
                   y_ref, nk_ref, nv_ref, nc_ref,
                   u_scr, us_scr, q_scr, att_scr):
    x = x_ref[...]
    k, v, _ = _split_proj(_in_proj(x, g_mix_ref, w_in_ref), rope_ref, u_scr, 0, q_scr)
    nk_ref[...] = k
    nv_ref[...] = v

    stride = HIST + n_tok
    dcs = []
    for b in range(n_batch):
        base = b * stride
        us_scr[base:base + HIST_PAD, :] = jnp.zeros((HIST_PAD, CONV_CH), F32)
        us_scr[base + HIST_PAD:base + HIST, :] = state_ref[b]
        us_scr[base + HIST:base + stride, :] = u_scr[b * n_tok:(b + 1) * n_tok, :]
        dcs.append(jnp.concatenate(
            _depthwise_conv(us_scr, base, n_tok, conv_w_ref, conv_b_ref), axis=-1))
        nc_ref[b] = us_scr[base + stride - (CONV_WIDTH - 1):base + stride, :]
    dc = jnp.concatenate(dcs, axis=0)

    kb = k.astype(BF16)
    vb = v.astype(BF16)
    for b in range(n_batch):
        r0 = b * n_tok
        for kh in range(N_KV_HEADS):
            hs = slice(kh * HEAD_DIM, (kh + 1) * HEAD_DIM)
            k_win = jnp.concatenate([ck_ref[b][:, hs].astype(BF16), kb[r0:r0 + n_tok, hs]], axis=0)
            v_win = jnp.concatenate([cv_ref[b][:, hs].astype(BF16), vb[r0:r0 + n_tok, hs]], axis=0)
            _attend(q_scr, r0, n_tok, kh, k_win, v_win, sink_ref, att_scr)

    h = _out_proj(x, dc, att_scr[...], cn_g_ref, cn_b_ref, w_out_ref)
    h = _ffn(h, _rmsnorm(h, g_ffn_ref[...]).astype(BF16), w_ff1_ref, w_ff2_ref)
    y_ref[...] = _dense_post(h, p_ref[...], g_ple_ref, w_gate_ref, w_proj_ref, g_fin_ref)


def _rope_tables(first_pos, count, repeats=1):
    half = HEAD_DIM // 2
    inv = 1.0 / (ROPE_THETA ** (np.arange(half, dtype=np.float64) / half))
    ang = (first_pos + np.arange(count, dtype=np.float64))[:, None] * inv[None, :]
    table = np.concatenate([np.cos(ang), np.sin(ang)], axis=-1)
    return jnp.asarray(np.tile(table, (repeats, 1)), dtype=F32)


def _resident(shape):
    nd = len(shape)
    return pl.BlockSpec(shape, lambda *_: (0,) * nd, pipeline_mode=pl.Buffered(1))


def _weight_specs():
    shapes = [(1, D_MODEL), (D_MODEL, IN_COLS), (CONV_WIDTH, CONV_CH), (1, CONV_CH), (1, CONV_CH),
              (1, CONV_CH), (MIX_COLS, D_MODEL + LANES), (1, D_MODEL), (D_MODEL, D_FF + LANES),
              (D_FF, D_MODEL + LANES), (1, D_MODEL), (D_MODEL, D_MODEL + LANES),
              (D_PLE, D_MODEL + LANES), (1, D_MODEL)]
    return [_resident(s) for s in shapes]


def kernel(x_prompt, x_sample, p_prompt, p_sample, cache_k, cache_v, state_conv, ln_mix_g, w_in, conv_w, conv_b, conv_norm_g, conv_norm_b, attn_sink, w_out, ln_ffn_g, w_ff1, w_ff2, ple_norm_g, w_ple_gate, w_ple_proj, final_norm_g):
    assert ln_mix_g.shape[0] == 1, "single-layer kernel"
    n_b, seq, _ = x_prompt.shape
    n_db, n_tok, _ = x_sample.shape
    assert seq % TILE == 0 and TILE % CHUNK == 0 and TILE >= WINDOW and TILE >= HIST

    row = lambda a: a.reshape(1, -1).astype(F32)
    lane_pad = lambda w: jnp.pad(w.astype(BF16), ((0, 0), (0, LANES)))
    weights = (row(ln_mix_g[0]), w_in[0].astype(BF16), conv_w[0], row(conv_b[0]),
               row(conv_norm_g[0]), row(conv_norm_b[0]), lane_pad(w_out[0]), row(ln_ffn_g[0]),
               lane_pad(w_ff1[0]), lane_pad(w_ff2[0]), row(ple_norm_g[0]),
               lane_pad(w_ple_gate[0]), lane_pad(w_ple_proj[0]), row(final_norm_g))
    sink = attn_sink[0].astype(F32)
    smem = pl.BlockSpec(memory_space=pltpu.SMEM)

    tps = seq // TILE
    n_tiles = n_b * tps
    proj_tile = lambda s: jnp.minimum(s, n_tiles - 1)
    mixer_tile = lambda s: jnp.clip(s - 1, 0, n_tiles - 1)
    dense_tile = lambda s: jnp.maximum(s - 2, 0)
    rope_p = _rope_tables(0, seq)
    tok = lambda width, tile: pl.BlockSpec(
        (None, TILE, width), lambda s: (tile(s) // tps, tile(s) % tps, 0))
    tab = pl.BlockSpec((TILE, HEAD_DIM), lambda s: (mixer_tile(s) % tps, 0))
    per_seq = lambda rows, width: pl.BlockSpec(
        (None, rows, width), lambda s: (mixer_tile(s) // tps, 0, 0))
    y_p, nk_p, nv_p, nc_p = pl.pallas_call(
        functools.partial(_prompt_kernel, tps, n_tiles),
        grid=(n_tiles + 2,),
        in_specs=[smem, tok(D_MODEL, proj_tile), tok(D_MODEL, mixer_tile), tok(D_PLE, dense_tile),
                  tab] + _weight_specs(),
        out_specs=[tok(D_MODEL, dense_tile), per_seq(WINDOW, KV_COLS), per_seq(WINDOW, KV_COLS),
                   per_seq(CONV_WIDTH - 1, CONV_CH)],
        out_shape=[jax.ShapeDtypeStruct((n_b, seq, D_MODEL), F32),
                   jax.ShapeDtypeStruct((n_b, WINDOW, KV_COLS), F32),
                   jax.ShapeDtypeStruct((n_b, WINDOW, KV_COLS), F32),
                   jax.ShapeDtypeStruct((n_b, CONV_WIDTH - 1, CONV_CH), F32)],
        scratch_shapes=[pltpu.VMEM((2, HIST + TILE, CONV_CH), F32),
                        pltpu.VMEM((TILE, Q_COLS), BF16),
                        pltpu.VMEM((2, N_KV_HEADS, 2, WINDOW + TILE, LANES), BF16),
                        pltpu.VMEM((2, N_KV_HEADS, 2, WINDOW + TILE, 2 * LANES), BF16),
                        pltpu.VMEM((TILE, Q_COLS), BF16),
                        pltpu.VMEM((TILE, IN_COLS), F32),
                        pltpu.VMEM((TILE, D_MODEL), BF16),
                        pltpu.VMEM((TILE, D_MODEL), F32)],
        compiler_params=pltpu.CompilerParams(
            dimension_semantics=("arbitrary",), vmem_limit_bytes=VMEM_LIMIT_BYTES),
        name="prompt_layer",
    )(sink, x_prompt, x_prompt, p_prompt[0], rope_p, *weights)

    rows = n_db * n_tok
    rope_s = _rope_tables(PAST_LEN, n_tok, repeats=n_db)
    full = lambda shape: pl.BlockSpec(shape, lambda i: (0,) * len(shape))
    y_s, nk_s, nv_s, nc_s = pl.pallas_call(
        functools.partial(_sample_kernel, n_db, n_tok),
        grid=(1,),
        in_specs=[smem, full((rows, D_MODEL)), full((rows, D_PLE)), full((rows, HEAD_DIM)),
                  full((n_db, WINDOW, KV_COLS)), full((n_db, WINDOW, KV_COLS)),
                  full((n_db, CONV_WIDTH - 1, CONV_CH))] + _weight_specs(),
        out_specs=[full((rows, D_MODEL)), full((rows, KV_COLS)), full((rows, KV_COLS)),
                   full((n_db, CONV_WIDTH - 1, CONV_CH))],
        out_shape=[jax.ShapeDtypeStruct((rows, D_MODEL), F32),
                   jax.ShapeDtypeStruct((rows, KV_COLS), F32),
                   jax.ShapeDtypeStruct((rows, KV_COLS), F32),
                   jax.ShapeDtypeStruct((n_db, CONV_WIDTH - 1, CONV_CH), F32)],
        scratch_shapes=[pltpu.VMEM((rows, CONV_CH), F32),
                        pltpu.VMEM((n_db * (HIST + n_tok), CONV_CH), F32),
                        pltpu.VMEM((rows, Q_COLS), F32),
                        pltpu.VMEM((rows, Q_COLS), F32)],
        compiler_params=pltpu.CompilerParams(dimension_semantics=("arbitrary",),
                                             vmem_limit_bytes=VMEM_LIMIT_BYTES),
        name="sample_layer",
    )(sink, x_sample.reshape(rows, D_MODEL), p_sample[0].reshape(rows, D_PLE),
      rope_s,
      cache_k[0].reshape(n_db, WINDOW, KV_COLS), cache_v[0].reshape(n_db, WINDOW, KV_COLS),
      state_conv[0], *weights)

    kv5 = lambda a, n, t: a.reshape(1, n, t, N_KV_HEADS, HEAD_DIM)
    return (y_p, y_s.reshape(n_db, n_tok, D_MODEL),
            kv5(nk_p, n_b, WINDOW), kv5(nv_p, n_b, WINDOW), nc_p[None],
            kv5(nk_s, n_db, n_tok), kv5(nv_s, n_db, n_tok), nc_s[None])
```
